```python
import jax, jax.numpy as jnp
from jax import lax
import numpy as np

D_MODEL = 1024
BATCH = 4
SEQ = 8192
DEPTH = 1
DEC_BATCH = 8
DEC_SEQ = 64
PAST_LEN = 2048

CHUNK = 64
A_WIDTH = D_MODEL
A_GROUPS = 8
A_GROUP_DIM = A_WIDTH // A_GROUPS
A_SPATIAL = 128
B_HEADS = 16
B_HEAD_DIM = 64
B_WIDTH = B_HEADS * B_HEAD_DIM
Q_BLOCK = 128
FORGET_BIAS_INIT = 3.0
N_EXPERTS = 32
TOP_K = 4
D_EXPERT = D_MODEL
SWIGLU_ALPHA = 1.702
SWIGLU_LIMIT = 7.0
MOE_BLOCK = 128
PLE_DIM = 256
DN_ALPHA = (2 * DEPTH) ** 0.25
DN_BETA = (8 * DEPTH) ** -0.25
LN_EPS = 1e-5

OFF_U = 0
OFF_V = OFF_U + A_WIDTH
OFF_Q = OFF_V + A_WIDTH
OFF_K = OFF_Q + B_WIDTH
OFF_VB = OFF_K + B_WIDTH
OFF_F = OFF_VB + B_WIDTH
OFF_GA = OFF_F + B_HEADS
OFF_GB = OFF_GA + D_MODEL
IN_WIDTH = OFF_GB + D_MODEL

kernel_name = "streaming_gmlp_fox_moe_encoder_step"


def layer_norm(x, g, b):
    xf = x.astype(jnp.float32)
    mu = jnp.mean(xf, axis=-1, keepdims=True)
    var = jnp.mean(jnp.square(xf - mu), axis=-1, keepdims=True)
    return ((xf - mu) * lax.rsqrt(var + LN_EPS) * g.astype(jnp.float32) + b.astype(jnp.float32)).astype(x.dtype)


def spatial_gate(v, w_s, b_s):
    L = v.shape[2]
    w = w_s[:, :L, :L] * jnp.tril(jnp.ones((L, L), w_s.dtype))
    return jnp.einsum('gts,bnsgc->bntgc', w, v) + b_s[:, :L].T[None, None, :, :, None]


def fox_attend(q, k, v, fq, fk, q_pos, k_pos):
    s = jnp.einsum('bqhd,bkhd->bhqk', q, k).astype(jnp.float32) * (B_HEAD_DIM ** -0.5)
    s = s + jnp.swapaxes(fq, 1, 2)[:, :, :, None] - jnp.swapaxes(fk, 1, 2)[:, :, None, :]
    s = jnp.where(k_pos[None, :] <= q_pos[:, None], s, -1e30)
    p = jax.nn.softmax(s, axis=-1)
    return jnp.einsum('bhqk,bkhd->bqhd', p.astype(v.dtype), v)


def fox_prompt(q, k, v, logf):
    Bn, S = q.shape[0], q.shape[1]
    F = jnp.cumsum(logf, axis=1)
    pos = jnp.arange(S, dtype=jnp.int32)
    nb = S // Q_BLOCK
    blocks = (jnp.swapaxes(q.reshape(Bn, nb, Q_BLOCK, B_HEADS, B_HEAD_DIM), 0, 1),
              jnp.swapaxes(F.reshape(Bn, nb, Q_BLOCK, B_HEADS), 0, 1),
              pos.reshape(nb, Q_BLOCK))
    out = lax.map(lambda a: fox_attend(a[0], k, v, a[1], F, a[2], pos), blocks)
    return jnp.swapaxes(out, 0, 1).reshape(Bn, S, B_WIDTH)


def fox_sample(q, k, v, logf, cache_k, cache_v, cache_logf):
    Bn, L = q.shape[0], q.shape[1]
    P = cache_k.shape[1]
    k_all = jnp.concatenate([cache_k, k.astype(cache_k.dtype)], axis=1)
    v_all = jnp.concatenate([cache_v, v.astype(cache_v.dtype)], axis=1)
    F = jnp.cumsum(jnp.concatenate([cache_logf.astype(jnp.float32), logf], axis=1), axis=1)
    k_pos = jnp.arange(P + L, dtype=jnp.int32)
    out = fox_attend(q, k_all, v_all, F[:, P:], F, k_pos[P:], k_pos)
    return out.reshape(Bn, L, B_WIDTH)


def clamped_swiglu_expert(xb, w_gu, b_gu, w_dn, b_dn):
    h = xb @ w_gu + b_gu
    gate = jnp.minimum(h[:, :D_EXPERT], SWIGLU_LIMIT)
    up = jnp.clip(h[:, D_EXPERT:], -SWIGLU_LIMIT, SWIGLU_LIMIT)
    glu = gate * jax.nn.sigmoid(gate * SWIGLU_ALPHA)
    return ((up + 1.0) * glu) @ w_dn + b_dn


def moe_ffn(x2d, w_router, b_router, w_gu, b_gu, w_dn, b_dn):
    T, D = x2d.shape
    logits = x2d.astype(jnp.float32) @ w_router.astype(jnp.float32) + b_router.astype(jnp.float32)
    top_v, top_i = lax.top_k(logits, TOP_K)
    gates = jax.nn.softmax(top_v, axis=-1)
    n_assign = T * TOP_K
    flat_e = top_i.reshape(n_assign)
    flat_tok = jnp.arange(n_assign, dtype=jnp.int32) // TOP_K
    order = jnp.argsort(flat_e)
    sorted_e = flat_e[order]
    counts = jnp.bincount(flat_e, length=N_EXPERTS)
    padded = (counts + MOE_BLOCK - 1) // MOE_BLOCK * MOE_BLOCK
    start = jnp.cumsum(counts) - counts
    pad_end = jnp.cumsum(padded)
    pad_start = pad_end - padded
    dest = pad_start[sorted_e] + jnp.arange(n_assign, dtype=jnp.int32) - start[sorted_e]
    n_blocks = -(-n_assign // MOE_BLOCK) + N_EXPERTS
    n_rows = n_blocks * MOE_BLOCK
    row_tok = jnp.zeros((n_rows,), jnp.int32).at[dest].set(flat_tok[order])
    row_gate = jnp.zeros((n_rows,), jnp.float32).at[dest].set(gates.reshape(n_assign)[order])
    block_e = jnp.minimum(jnp.searchsorted(pad_end, jnp.arange(n_blocks, dtype=jnp.int32) * MOE_BLOCK, side='right'), N_EXPERTS - 1)
    xs = x2d[row_tok].reshape(n_blocks, MOE_BLOCK, D)

    def run_block(args):
        xb, e = args
        return clamped_swiglu_expert(xb, w_gu[e], b_gu[e], w_dn[e], b_dn[e])

    ys = lax.map(run_block, (xs, block_e))
    contrib = ys.reshape(n_rows, D) * row_gate[:, None].astype(x2d.dtype)
    return jnp.zeros_like(x2d).at[row_tok].add(contrib)


def encoder_layer(x, p, cache_k, cache_v, cache_logf, w_in, b_in, lnv_g, lnv_b, w_s, b_s, w_pa, w_pb, w_o,
                  ln1_g, ln1_b, w_router, b_router, w_gu, b_gu, w_dn, b_dn, w_ple_gate, b_ple_gate, w_ple_proj,
                  ln2_g, ln2_b):
    Bn, L, _ = x.shape
    proj = x @ w_in + b_in
    z = jax.nn.gelu(proj[..., OFF_U:OFF_Q])
    u = z[..., :A_WIDTH].reshape(Bn, L, A_GROUPS, A_GROUP_DIM)
    va = layer_norm(z[..., A_WIDTH:], lnv_g, lnv_b).reshape(Bn, L, A_GROUPS, A_GROUP_DIM)
    q = proj[..., OFF_Q:OFF_K].reshape(Bn, L, B_HEADS, B_HEAD_DIM)
    k = proj[..., OFF_K:OFF_VB].reshape(Bn, L, B_HEADS, B_HEAD_DIM)
    vb = proj[..., OFF_VB:OFF_F].reshape(Bn, L, B_HEADS, B_HEAD_DIM)
    logf = jax.nn.log_sigmoid(proj[..., OFF_F:OFF_GA].astype(jnp.float32))
    g_a = proj[..., OFF_GA:OFF_GB]
    g_b = proj[..., OFF_GB:IN_WIDTH]
    if cache_k is None:
        sg = spatial_gate(va.reshape(Bn, L // A_SPATIAL, A_SPATIAL, A_GROUPS, A_GROUP_DIM), w_s, b_s)
        sg = sg.reshape(Bn, L, A_GROUPS, A_GROUP_DIM)
        attn = fox_prompt(q, k, vb, logf)
    else:
        sg = spatial_gate(va[:, None], w_s, b_s)[:, 0]
        attn = fox_sample(q, k, vb, logf, cache_k, cache_v, cache_logf)
    a_out = (u * sg).reshape(Bn, L, A_WIDTH)
    merged = jax.nn.sigmoid(g_a) * (a_out @ w_pa) + jax.nn.sigmoid(g_b) * (attn.astype(x.dtype) @ w_pb)
    x1 = layer_norm(DN_ALPHA * x + merged @ w_o, ln1_g, ln1_b)
    ffn = moe_ffn(x1.reshape(Bn * L, D_MODEL), w_router, b_router, w_gu, b_gu, w_dn, b_dn).reshape(Bn, L, D_MODEL)
    ple = jax.nn.sigmoid(x1 @ w_ple_gate + b_ple_gate) * (p @ w_ple_proj)
    y = layer_norm(DN_ALPHA * x1 + ffn + ple, ln2_g, ln2_b)
    return y, k, vb, logf, va


def setup_inputs(seed: int = 0) -> dict:
    key = jax.random.key(seed)
    ks = jax.random.split(key, 40)
    f32 = jnp.float32
    nrm = lambda k, s, sc: jax.random.normal(k, s, f32) * sc
    b_in = nrm(ks[11], (DEPTH, IN_WIDTH), 0.02).at[:, OFF_F:OFF_GA].add(FORGET_BIAS_INIT)
    return {
        "x_prompt": nrm(ks[0], (BATCH, SEQ, D_MODEL), 1.0),
        "x_sample": nrm(ks[1], (DEC_BATCH, DEC_SEQ, D_MODEL), 1.0),
        "cache_fox_k": nrm(ks[2], (DEPTH, DEC_BATCH, PAST_LEN, B_HEADS, B_HEAD_DIM), 1.0),
        "cache_fox_v": nrm(ks[3], (DEPTH, DEC_BATCH, PAST_LEN, B_HEADS, B_HEAD_DIM), 1.0),
        "cache_fox_logf": jax.nn.log_sigmoid(FORGET_BIAS_INIT + nrm(ks[4], (DEPTH, DEC_BATCH, PAST_LEN, B_HEADS), 0.5)),
        "p_prompt": nrm(ks[5], (DEPTH, BATCH, SEQ, PLE_DIM), 1.0),
        "p_sample": nrm(ks[6], (DEPTH, DEC_BATCH, DEC_SEQ, PLE_DIM), 1.0),
        "w_in": nrm(ks[10], (DEPTH, D_MODEL, IN_WIDTH), D_MODEL ** -0.5),
        "b_in": b_in,
        "lnv_g": 1.0 + nrm(ks[12], (DEPTH, A_WIDTH), 0.02),
        "lnv_b": nrm(ks[13], (DEPTH, A_WIDTH), 0.02),
        "w_s": nrm(ks[14], (DEPTH, A_GROUPS, A_SPATIAL, A_SPATIAL), A_SPATIAL ** -0.5),
        "b_s": 1.0 + nrm(ks[15], (DEPTH, A_GROUPS, A_SPATIAL), 0.02),
        "w_pa": nrm(ks[16], (DEPTH, A_WIDTH, D_MODEL), A_WIDTH ** -0.5 * DN_BETA),
        "w_pb": nrm(ks[17], (DEPTH, B_WIDTH, D_MODEL), B_WIDTH ** -0.5 * DN_BETA),
        "w_o": nrm(ks[18], (DEPTH, D_MODEL, D_MODEL), D_MODEL ** -0.5 * DN_BETA),
        "ln1_g": 1.0 + nrm(ks[19], (DEPTH, D_MODEL), 0.02),
        "ln1_b": nrm(ks[20], (DEPTH, D_MODEL), 0.02),
        "w_router": nrm(ks[21], (DEPTH, D_MODEL, N_EXPERTS), D_MODEL ** -0.5),
        "b_router": nrm(ks[22], (DEPTH, N_EXPERTS), 0.01),
        "w_gu": nrm(ks[23], (DEPTH, N_EXPERTS, D_MODEL, 2 * D_EXPERT), D_MODEL ** -0.5),
        "b_gu": nrm(ks[24], (DEPTH, N_EXPERTS, 2 * D_EXPERT), 0.02),
        "w_dn": nrm(ks[25], (DEPTH, N_EXPERTS, D_EXPERT, D_MODEL), D_EXPERT ** -0.5 * DN_BETA),
        "b_dn": nrm(ks[26], (DEPTH, N_EXPERTS, D_MODEL), 0.02),
        "w_ple_gate": nrm(ks[27], (DEPTH, D_MODEL, D_MODEL), D_MODEL ** -0.5),
        "b_ple_gate": nrm(ks[28], (DEPTH, D_MODEL), 0.02),
        "w_ple_proj": nrm(ks[29], (DEPTH, PLE_DIM, D_MODEL), PLE_DIM ** -0.5 * DN_BETA),
        "ln2_g": 1.0 + nrm(ks[30], (DEPTH, D_MODEL), 0.02),
        "ln2_b": nrm(ks[31], (DEPTH, D_MODEL), 0.02),
    }


def reference(x_prompt, x_sample, cache_fox_k, cache_fox_v, cache_fox_logf, p_prompt, p_sample,
              w_in, b_in, lnv_g, lnv_b, w_s, b_s, w_pa, w_pb, w_o, ln1_g, ln1_b,
              w_router, b_router, w_gu, b_gu, w_dn, b_dn, w_ple_gate, b_ple_gate, w_ple_proj, ln2_g, ln2_b):
    hp, hs = x_prompt, x_sample
    kp, vp, fp, ksl, vsl, fsl, gsl = [], [], [], [], [], [], []
    for i in range(DEPTH):
        w = (w_in[i], b_in[i], lnv_g[i], lnv_b[i], w_s[i], b_s[i], w_pa[i], w_pb[i], w_o[i], ln1_g[i], ln1_b[i],
             w_router[i], b_router[i], w_gu[i], b_gu[i], w_dn[i], b_dn[i], w_ple_gate[i], b_ple_gate[i],
             w_ple_proj[i], ln2_g[i], ln2_b[i])
        hp, k1, v1, f1, _ = encoder_layer(hp, p_prompt[i], None, None, None, *w)
        hs, k2, v2, f2, g2 = encoder_layer(hs, p_sample[i], cache_fox_k[i], cache_fox_v[i], cache_fox_logf[i], *w)
        kp.append(k1); vp.append(v1); fp.append(f1)
        ksl.append(k2); vsl.append(v2); fsl.append(f2); gsl.append(g2)
    new_k_prompt = jnp.stack(kp)
    new_v_prompt = jnp.stack(vp)
    new_logf_prompt = jnp.stack(fp)
    new_k_sample = jnp.stack(ksl)
    new_v_sample = jnp.stack(vsl)
    new_logf_sample = jnp.stack(fsl)
    new_gmlp_v_sample = jnp.stack(gsl)
    return (hp, hs, new_k_prompt, new_v_prompt, new_logf_prompt, new_k_sample, new_v_sample, new_logf_sample, new_gmlp_v_sample)
```

```python
import functools

import jax
import jax.numpy as jnp
import numpy as np
from jax import lax
from jax.experimental import pallas as pl
from jax.experimental.pallas import tpu as pltpu

F32 = jnp.float32
BF16 = jnp.bfloat16
U32 = jnp.uint32
I32 = jnp.int32

LANES = 128
D_MODEL = 1024
N_HEADS = 16
HEAD_DIM = 64
N_GROUPS = 8
N_EXPERTS = 32
TOP_K = 4
SWIGLU_ALPHA = 1.702
SWIGLU_LIMIT = 7.0
LN_EPS = 1e-5
NEG_BIG = -1e30

TOKEN_TILE = 256
Q_TILE = 512
EXPERT_BLOCK = 256
VMEM_LIMIT = 56 * 1024 * 1024

F_MID_SHIFT = N_HEADS
F_LO_SHIFT = 2 * N_HEADS
F_ONE_LANE = 3 * N_HEADS


def _dot(a, b):
    return jnp.dot(a, b, preferred_element_type=F32)


def _dot_nt(a, b):
    return lax.dot_general(a, b, (((1,), (1,)), ((), ())), preferred_element_type=F32)


def _layer_norm(x, g, b):
    mu = jnp.mean(x, axis=-1, keepdims=True)
    xc = x - mu
    var = jnp.mean(xc * xc, axis=-1, keepdims=True)
    return xc * lax.rsqrt(var + LN_EPS) * g + b


def _log_sigmoid(z):
    return jnp.minimum(z, 0.0) - jnp.log1p(jnp.exp(-jnp.abs(z)))


def _split3(x):
    hi = x.astype(BF16).astype(F32)
    r = x - hi
    mid = r.astype(BF16).astype(F32)
    lo = (r - mid).astype(BF16).astype(F32)
    return hi, mid, lo


def _tri(n, strict):
    r = lax.broadcasted_iota(I32, (n, n), 0)
    c = lax.broadcasted_iota(I32, (n, n), 1)
    keep = (r > c) if strict else (r >= c)
    return jnp.where(keep, 1.0, 0.0).astype(F32)


def _cumsum_rows(x):
    n = x.shape[0]
    hi, mid, lo = _split3(x)
    cat = jnp.concatenate([hi, mid, lo], axis=1).astype(BF16)
    c = _dot(_tri(n, strict=False).astype(BF16), cat)
    return (c[:, 2 * LANES:] + c[:, LANES:2 * LANES]) + c[:, :LANES]


def _pack_bf16_pairs(x):
    w = x.shape[1] // 2
    xb = x.astype(BF16).astype(F32)
    lo = pltpu.bitcast(xb[:, :w], U32) >> 16
    hi = pltpu.bitcast(xb[:, w:], U32) & jnp.uint32(0xFFFF0000)
    return hi | lo


def _unpack_bf16_pairs(wd):
    lo = pltpu.bitcast(wd << 16, F32)
    hi = pltpu.bitcast(wd & jnp.uint32(0xFFFF0000), F32)
    return lo, hi


def _proj_sections(x_ref, wuv, wkv, wf, wg, buv, bkv, bf_, bg, lng, lnb, u_o, k_o, v_o, g_o):
    xb = x_ref[...].astype(BF16)
    pu = _dot(xb, wuv[:, :D_MODEL]) + buv[:, :D_MODEL]
    u_o[...] = jax.nn.gelu(pu).astype(BF16)
    pv = _dot(xb, wuv[:, D_MODEL:]) + buv[:, D_MODEL:]
    va = _layer_norm(jax.nn.gelu(pv), lng[...], lnb[...])
    k = _dot(xb, wkv[:, :D_MODEL]) + bkv[:, :D_MODEL]
    k_o[...] = k
    v =_dot(xb, wkv[:, D_MODEL:]) + bkv[:, D_MODEL:]
    v_o[...] = v
    g_o[...] = jax.nn.sigmoid(_dot(xb, wg[...]) + bg[...]).astype(BF16)
    f = _dot(xb, wf[...]) + bf_[...]
    lane = lax.broadcasted_iota(I32, f.shape, 1)
    lf = jnp.where(lane < N_HEADS, _log_sigmoid(f), 0.0)
    return xb, va, k, v, lf


def _proj_prompt_kernel(x_ref, wuv, wq, wkv, wf, wg, buv, bq, bkv, bf_, bg, lng, lnb, eq, ek,
                        u_o, va_o, k_o, v_o, lf_o, g_o, qa_o, ka_o, vt_o, carry, *, tiles_per_seq):
    i = pl.program_id(0)
    xb, va, k, v, lf = _proj_sections(x_ref, wuv, wkv, wf, wg, buv, bkv, bf_, bg, lng, lnb, u_o, k_o, v_o, g_o)
    va_o[...] = va.astype(BF16)
    vt_o[0, 0] = v.T.astype(BF16)
    lf_o[...] = lf[:, :N_HEADS]

    @pl.when(i % tiles_per_seq == 0)
    def _():
        carry[...] = jnp.zeros_like(carry)

    tm = lf.shape[0]
    fcum = carry[...] + _cumsum_rows(lf)
    carry[...] = fcum[tm - 1:tm, :]

    hi, mid, lo = _split3(fcum)
    lane = lax.broadcasted_iota(I32, fcum.shape, 1)
    pack = (hi + pltpu.roll(mid, F_MID_SHIFT, 1) + pltpu.roll(lo, F_LO_SHIFT, 1)
            + jnp.where(lane == F_ONE_LANE, 1.0, 0.0)).astype(BF16)
    add_q = _dot(pack, eq[...])
    add_k = _dot(pack, ek[...])
    q = (_dot(xb, wq[...]) + bq[...]) * (HEAD_DIM ** -0.5)
    low =lane < HEAD_DIM
    for g in range(N_GROUPS):
        sl = slice(g * LANES, (g + 1) * LANES)
        ev = slice(2 * g * LANES, (2 * g + 1) * LANES)
        od = slice((2 * g + 1) * LANES, (2 * g + 2) * LANES)
        qa_o[:, ev] = jnp.where(low, q[:, sl], add_q[:, ev]).astype(BF16)
        qa_o[:, od] = jnp.where(low, add_q[:, od], q[:, sl]).astype(BF16)
        ka_o[:, ev] = jnp.where(low, k[:, sl], add_k[:, ev]).astype(BF16)
        ka_o[:, od] = jnp.where(low, add_k[:, od], k[:, sl]).astype(BF16)


def _proj_sample_kernel(x_ref, wuv, wq, wkv, wf, wg, buv, bq, bkv, bf_, bg, lng, lnb,
                        u_o, va_o, k_o, v_o, lf_o, g_o, q_o):
    xb, va, _, _, lf = _proj_sections(x_ref, wuv, wkv, wf, wg, buv, bkv, bf_, bg, lng, lnb, u_o, k_o, v_o, g_o)
    va_o[...] = va
    lf_o[...] = lf
    q_o[...] = (_dot(xb, wq[...]) + bq[...]) * (HEAD_DIM ** -0.5)


def _const_spec(shape):
    nd = len(shape)
    return pl.BlockSpec(shape, lambda *_: (0,) * nd, pipeline_mode=pl.Buffered(1))


def _proj_weights(w_in, b_in):
    d = D_MODEL
    off_q, off_k, off_f = 2 * d, 3 * d, 5 * d
    off_ga = off_f + N_HEADS
    wuv = w_in[:, :off_q].astype(BF16)
    wq = w_in[:, off_q:off_k].astype(BF16)
    wkv = w_in[:, off_k:off_f].astype(BF16)
    wf = jnp.pad(w_in[:, off_f:off_ga], ((0, 0), (0, LANES - N_HEADS))).astype(BF16)
    wg = w_in[:, off_ga:].astype(BF16)
    b = b_in.reshape(1, -1).astype(F32)
    buv, bq, bkv = b[:, :off_q], b[:, off_q:off_k], b[:, off_k:off_f]
    bf_ = jnp.pad(b[:, off_f:off_ga], ((0, 0), (0, LANES - N_HEADS)))
    bg = b[:, off_ga:]
    return (wuv, wq, wkv, wf, wg), (buv, bq, bkv, bf_, bg)


def _spread_matrices():
    eq_np = np.zeros((LANES, 2 * D_MODEL), np.float32)
    ek_np = np.zeros((LANES, 2 * D_MODEL), np.float32)
    for h in range(N_HEADS):
        base = LANES * h + (HEAD_DIM if h % 2 == 0 else 0)
        for c, shift in enumerate((0, F_MID_SHIFT, F_LO_SHIFT)):
            eq_np[shift + h, base + c] = 1.0
            eq_np[F_ONE_LANE, base + 3 + c] = 1.0
            ek_np[F_ONE_LANE, base + c] = 1.0
            ek_np[shift + h, base + 3 + c] = -1.0
    return jnp.asarray(eq_np, BF16), jnp.asarray(ek_np, BF16)


def _project_prompt(x2d, seq, ws, bs, lng, lnb):
    t = x2d.shape[0]
    tm = min(TOKEN_TILE, seq)
    nt = t // tm
    nb = t // seq
    eq, ek = _spread_matrices()
    consts = list(ws) + list(bs) + [lng, lnb, eq, ek]
    row = lambda w: pl.BlockSpec((tm, w), lambda i: (i, 0))
    out_shape = (
        jax.ShapeDtypeStruct((t, D_MODEL), BF16),
        jax.ShapeDtypeStruct((t, D_MODEL), BF16),
        jax.ShapeDtypeStruct((t, D_MODEL), F32),
        jax.ShapeDtypeStruct((t, D_MODEL), F32),
        jax.ShapeDtypeStruct((t, N_HEADS), F32),
        jax.ShapeDtypeStruct((t, 2 * D_MODEL), BF16),
        jax.ShapeDtypeStruct((t, 2 * D_MODEL), BF16),
        jax.ShapeDtypeStruct((t, 2 * D_MODEL), BF16),
        jax.ShapeDtypeStruct((nb, seq // tm, D_MODEL, tm), BF16),
    )
    tiles_per_seq = seq // tm
    out_specs = (row(D_MODEL), row(D_MODEL), row(D_MODEL), row(D_MODEL), row(N_HEADS),
                 row(2 * D_MODEL), row(2 * D_MODEL), row(2 * D_MODEL),
                 pl.BlockSpec((1, 1, D_MODEL, tm), lambda i: (i // tiles_per_seq, i % tiles_per_seq, 0, 0)))
    return pl.pallas_call(
        functools.partial(_proj_prompt_kernel, tiles_per_seq=tiles_per_seq),
        grid=(nt,),
        in_specs=[row(D_MODEL)] + [_const_spec(c.shape) for c in consts],
        out_specs=out_specs,
        out_shape=out_shape,
        scratch_shapes=[pltpu.VMEM((1, LANES), F32)],
        compiler_params=pltpu.CompilerParams(dimension_semantics=("arbitrary",), vmem_limit_bytes=VMEM_LIMIT),
        name="proj_prompt",
    )(x2d, *consts)


def _project_sample(x2d, ws, bs, lng, lnb):
    t = x2d.shape[0]
    tm = min(TOKEN_TILE, t)
    nt = t // tm
    consts = list(ws) + list(bs) + [lng, lnb]
    row = lambda w: pl.BlockSpec((tm, w), lambda i: (i, 0))
    out_shape = (
        jax.ShapeDtypeStruct((t, D_MODEL), BF16),
        jax.ShapeDtypeStruct((t, D_MODEL), F32),
        jax.ShapeDtypeStruct((t, D_MODEL), F32),
        jax.ShapeDtypeStruct((t, D_MODEL), F32),
        jax.ShapeDtypeStruct((t, LANES), F32),
        jax.ShapeDtypeStruct((t, 2 * D_MODEL), BF16),
        jax.ShapeDtypeStruct((t, D_MODEL), F32),
    )
    out_specs = (row(D_MODEL), row(D_MODEL), row(D_MODEL), row(D_MODEL), row(LANES), row(2 * D_MODEL), row(D_MODEL))
    return pl.pallas_call(
        _proj_sample_kernel,
        grid=(nt,),
        in_specs=[row(D_MODEL)] + [_const_spec(c.shape) for c in consts],
        out_specs=out_specs,
        out_shape=out_shape,
        compiler_params=pltpu.CompilerParams(dimension_semantics=("arbitrary",), vmem_limit_bytes=VMEM_LIMIT),
        name="proj_sample",
    )(x2d, *consts)


def _attn_prompt_kernel(q_ref, k_ref, vt_ref, o_ref, m_ref, l_ref, acc_ref, ot_ref, *, tq, tk):
    qi = pl.program_id(2)
    ratio = tq // tk
    n_full = qi * ratio
    key_row = lax.broadcasted_iota(I32, (tk, tq), 0)
    qry_col = lax.broadcasted_iota(I32, (tk, tq), 1)
    for hh in range(2):
        q = q_ref[:, hh * LANES:(hh + 1) * LANES]
        m_ref[...] = jnp.full(m_ref.shape, NEG_BIG, F32)
        l_ref[...] = jnp.zeros(l_ref.shape, F32)
        acc_ref[...] = jnp.zeros(acc_ref.shape, F32)

        def step(kj, diag_offset):
            start = pl.multiple_of(kj * tk, tk)
            k = k_ref[pl.ds(start, tk), hh * LANES:(hh + 1) * LANES]
            st = _dot_nt(k, q)
            if diag_offset is not None:
                st = jnp.where(key_row + diag_offset * tk <= qry_col, st, NEG_BIG)
            m_old = m_ref[...]
            m_new = jnp.maximum(m_old, jnp.max(st, axis=0, keepdims=True))
            alpha = jnp.exp(m_old - m_new)
            p = jnp.exp(st - m_new)
            l_ref[...] = alpha * l_ref[...] + jnp.sum(p, axis=0, keepdims=True)
            vt = vt_ref[0, kj, hh * HEAD_DIM:(hh + 1) * HEAD_DIM, :]
            acc_ref[...] = alpha * acc_ref[...] + _dot(vt, p.astype(BF16))
            m_ref[...] = m_new

        def body(kj, c):
            step(kj, None)
            return c

        lax.fori_loop(0, n_full, body, 0)
        for d in range(ratio):
            step(n_full + d, d)
        ot_ref[hh * HEAD_DIM:(hh + 1) * HEAD_DIM, :] = acc_ref[...] / l_ref[...]
    o_ref[...] = ot_ref[...].T.astype(BF16)


def _attend_prompt(q_aug, k_aug, v_t, nb, seq):
    t = q_aug.shape[0]
    tk = v_t.shape[3]
    tq = min(Q_TILE, seq)
    nq = seq // tq
    return pl.pallas_call(
        functools.partial(_attn_prompt_kernel, tq=tq, tk=tk),
        grid=(nb, N_GROUPS, nq),
        in_specs=[
            pl.BlockSpec((tq, 2 * LANES), lambda b, g, i: (b * nq + i, g)),
            pl.BlockSpec((seq, 2 * LANES), lambda b, g, i: (b, g)),
            pl.BlockSpec((1, seq // tk, LANES, tk), lambda b, g, i: (b, 0, g, 0)),
        ],
        out_specs=pl.BlockSpec((tq, LANES), lambda b, g, i: (b * nq + i, g)),
        out_shape=jax.ShapeDtypeStruct((t, D_MODEL), BF16),
        scratch_shapes=[pltpu.VMEM((1, tq), F32), pltpu.VMEM((1, tq), F32),
                        pltpu.VMEM((HEAD_DIM, tq), F32), pltpu.VMEM((LANES, tq), F32)],
        compiler_params=pltpu.CompilerParams(
            dimension_semantics=("arbitrary", "arbitrary", "arbitrary"), vmem_limit_bytes=VMEM_LIMIT),
        name="attn_prompt",
    )(q_aug, k_aug, v_t)


def _attn_sample_kernel(q_ref, k_ref, v_ref, lf_ref, ck_ref, cv_ref, clf_ref, o_ref, *, chunk):
    past = ck_ref.shape[1]
    new = q_ref.shape[1]
    clf = clf_ref[0]
    carry = jnp.zeros((1, LANES), F32)
    parts = []
    for c in range(past // chunk):
        fc = carry + _cumsum_rows(clf[c * chunk:(c + 1) * chunk])
        carry = fc[chunk - 1:chunk, :]
        parts.append(fc)
    f_cache = jnp.concatenate(parts, axis=0)
    f_new = carry + _cumsum_rows(lf_ref[0])
    f_cache_t = f_cache.T
    f_new_t = f_new.T
    lane = lax.broadcasted_iota(I32, (new, LANES), 1)
    causal = lax.broadcasted_iota(I32, (new, new), 1) <= lax.broadcasted_iota(I32, (new, new), 0)
    for g in range(N_GROUPS):
        sl = slice(g * LANES, (g + 1) * LANES)
        qg = q_ref[0, :, sl]
        kc = ck_ref[0, :, sl].astype(BF16)
        vc = cv_ref[0, :, sl].astype(BF16)
        kn = k_ref[0, :, sl].astype(BF16)
        vn = v_ref[0, :, sl].astype(BF16)
        outs = []
        for hh in range(2):
            h = 2 * g + hh
            mine = (lane < HEAD_DIM) if hh == 0 else (lane >= HEAD_DIM)
            qm = jnp.where(mine, qg, 0.0).astype(BF16)
            fq = f_new[:, h:h + 1]
            s_c = _dot_nt(qm, kc) + fq - f_cache_t[h:h + 1, :]
            s_n = _dot_nt(qm, kn) + fq - f_new_t[h:h + 1, :]
            s_n = jnp.where(causal, s_n, NEG_BIG)
            m = jnp.maximum(jnp.max(s_c, axis=1, keepdims=True), jnp.max(s_n, axis=1, keepdims=True))
            p_c = jnp.exp(s_c - m)
            p_n = jnp.exp(s_n - m)
            den = jnp.sum(p_c, axis=1, keepdims=True) + jnp.sum(p_n, axis=1, keepdims=True)
            outs.append((_dot(p_c.astype(BF16), vc) + _dot(p_n.astype(BF16), vn)) / den)
        o_ref[:, sl] = jnp.where(lane < HEAD_DIM, outs[0], outs[1]).astype(BF16)


def _attend_sample(q, k, v, lf, cache_k, cache_v, cache_lf):
    nb, new, _ = q.shape
    past = cache_k.shape[1]
    chunk = min(TOKEN_TILE, past)
    blk = lambda n, w: pl.BlockSpec((1, n, w), lambda b: (b, 0, 0))
    return pl.pallas_call(
        functools.partial(_attn_sample_kernel, chunk=chunk),
        grid=(nb,),
        in_specs=[blk(new, D_MODEL), blk(new, D_MODEL), blk(new, D_MODEL), blk(new, LANES),
                  blk(past, D_MODEL), blk(past, D_MODEL), blk(past, LANES)],
        out_specs=pl.BlockSpec((new, D_MODEL), lambda b: (b, 0)),
        out_shape=jax.ShapeDtypeStruct((nb * new, D_MODEL), BF16),
        compiler_params=pltpu.CompilerParams(dimension_semantics=("arbitrary",), vmem_limit_bytes=VMEM_LIMIT),
        name="attn_sample",
    )(q, k, v, lf, cache_k, cache_v, cache_lf)


def _post_kernel(x_ref, u_ref, va_ref, g_ref, at_ref, p_ref, base_ref, ws_ref, bst_ref, wpa, wpb, wo,
                 l1g, l1b, wr_hi, wr_lo, br, wpg, bpg, wpp,
                 x1p_o, resid_o, topi_o, gate_o, rank_o, cnt_o, a_sc, base_sc, *, chunk, dn_alpha):
    i = pl.program_id(0)
    tm = x_ref.shape[0]

    @pl.when(i == 0)
    def _():
        base_sc[...] = base_ref[...]

    tri = _tri(chunk, strict=False)
    for g in range(N_GROUPS):
        sl = slice(g * LANES, (g + 1) * LANES)
        wm = (ws_ref[g, :chunk, :chunk] * tri).astype(BF16)
        bias = bst_ref[:chunk, g:g + 1]
        for c in range(tm // chunk):
            rows = slice(c * chunk, (c + 1) * chunk)
            sg = _dot(wm, va_ref[rows, sl]) + bias
            a_sc[rows, sl] = (u_ref[rows, sl].astype(F32) * sg).astype(BF16)

    pa = _dot(a_sc[...], wpa[...])
    pb = _dot(at_ref[...], wpb[...])
    merged = g_ref[:, :D_MODEL].astype(F32) * pa + g_ref[:, D_MODEL:].astype(F32) * pb
    x1 = _layer_norm(dn_alpha * x_ref[...] + _dot(merged.astype(BF16), wo[...]), l1g[...], l1b[...])
    x1b = x1.astype(BF16)
    x1p_o[...] = _pack_bf16_pairs(x1)

    ple = jax.nn.sigmoid(_dot(x1b, wpg[...]) + bpg[...]) * _dot(p_ref[...].astype(BF16), wpp[...])
    resid_o[...] = dn_alpha * x1 + ple

    x1_lo = (x1 - x1b.astype(F32)).astype(BF16)
    logits = (_dot(x1b, wr_hi[...]) + (_dot(x1b, wr_lo[...]) + _dot(x1_lo, wr_hi[...]))) + br[...]
    lane = lax.broadcasted_iota(I32, (tm, LANES), 1)
    work = logits
    vals, idxs = [], []
    for _ in range(TOP_K):
        mk = jnp.max(work, axis=1, keepdims=True)
        ik = jnp.min(jnp.where(work == mk, lane, LANES), axis=1, keepdims=True)
        vals.append(mk)
        idxs.append(ik)
        work = jnp.where(lane == ik, -jnp.inf, work)
    exps = [jnp.exp(v - vals[0]) for v in vals]
    den = exps[0] + exps[1] + exps[2] + exps[3]

    lower = _tri(tm, strict=True).astype(BF16)
    run = base_sc[...]
    topi = jnp.zeros((tm, LANES), I32)
    gate = jnp.zeros((tm, LANES), F32)
    rank = jnp.zeros((tm, LANES), F32)
    for k in range(TOP_K):
        onehot = jnp.where(lane == idxs[k], 1.0, 0.0)
        before = _dot(lower, onehot.astype(BF16))
        rk = jnp.sum(onehot * (before + run), axis=1, keepdims=True)
        topi = jnp.where(lane == k, idxs[k], topi)
        gate = jnp.where(lane == k, exps[k] / den, gate)
        rank = jnp.where(lane == k, rk, rank)
        run = run + jnp.sum(onehot, axis=0, keepdims=True)
    base_sc[...] = run
    topi_o[...] = topi
    gate_o[...] = gate
    rank_o[...] = rank.astype(I32)
    cnt_o[...] = run


def _post_attention(x2d, u, va, gates, attn, p2d, base, consts, chunk, dn_alpha):
    t = x2d.shape[0]
    tm = min(TOKEN_TILE, t)
    nt = t // tm
    row = lambda w: pl.BlockSpec((tm, w), lambda i: (i, 0))
    ple_dim = p2d.shape[1]
    out_shape = (
        jax.ShapeDtypeStruct((t, D_MODEL // 2), U32),
        jax.ShapeDtypeStruct((t, D_MODEL), F32),
        jax.ShapeDtypeStruct((t, LANES), I32),
        jax.ShapeDtypeStruct((t, LANES), F32),
        jax.ShapeDtypeStruct((t, LANES), I32),
        jax.ShapeDtypeStruct((1, LANES), F32),
    )
    out_specs = (row(D_MODEL // 2), row(D_MODEL), row(LANES), row(LANES), row(LANES),
                 pl.BlockSpec((1, LANES), lambda i: (0, 0)))
    return pl.pallas_call(
        functools.partial(_post_kernel, chunk=chunk, dn_alpha=dn_alpha),
        grid=(nt,),
        in_specs=[row(D_MODEL), row(D_MODEL), row(D_MODEL), row(2 * D_MODEL), row(D_MODEL), row(ple_dim),
                  _const_spec(base.shape)] + [_const_spec(c.shape) for c in consts],
        out_specs=out_specs,
        out_shape=out_shape,
        scratch_shapes=[pltpu.VMEM((tm, D_MODEL), BF16), pltpu.VMEM((1, LANES), F32)],
        compiler_params=pltpu.CompilerParams(dimension_semantics=("arbitrary",), vmem_limit_bytes=VMEM_LIMIT),
        name="post_attention",
    )(x2d, u, va, gates, attn, p2d, base, *consts)


def _dispatch_kernel(x_ref, dest_ref, xs_in, xs_out, sem):
    del xs_in
    td = x_ref.shape[0]

    def body(i, c):
        tok = lax.shift_right_logical(i, TOP_K.bit_length() - 1)
        pltpu.make_async_copy(x_ref.at[pl.ds(tok, 1)], xs_out.at[pl.ds(dest_ref[i], 1)], sem).start()
        return c

    lax.fori_loop(0, td * TOP_K, body, 0)
    for _ in range(TOP_K):
        pltpu.make_async_copy(x_ref, xs_out.at[pl.ds(0, td)], sem).wait()


def _dispatch(x1p, dest, xs):
    t, w = x1p.shape
    td = min(TOKEN_TILE, t)
    return pl.pallas_call(
        _dispatch_kernel,
        grid=(t // td,),
        in_specs=[pl.BlockSpec((td, w), lambda i: (i, 0)),
                  pl.BlockSpec((td * TOP_K,), lambda i: (i,), memory_space=pltpu.SMEM),
                  pl.BlockSpec(memory_space=pl.ANY)],
        out_specs=pl.BlockSpec(memory_space=pl.ANY),
        out_shape=jax.ShapeDtypeStruct(xs.shape, xs.dtype),
        scratch_shapes=[pltpu.SemaphoreType.DMA(())],
        input_output_aliases={2: 0},
        compiler_params=pltpu.CompilerParams(dimension_semantics=("arbitrary",)),
        name="dispatch",
    )(x1p, dest, xs)


def _expert_kernel(be_ref, nu_ref, xs_ref, wgu, bgu, wdn, bdn, ys_o):
    del be_ref
    used = pl.program_id(0) < nu_ref[0]

    @pl.when(jnp.logical_not(used))
    def _():
        ys_o[...] = jnp.zeros(ys_o.shape, ys_o.dtype)

    @pl.when(used)
    def _():
        lo, hi = _unpack_bf16_pairs(xs_ref[...])
        x = jnp.concatenate([lo, hi], axis=1).astype(BF16)
        h = _dot(x, wgu[0]) + bgu[0]
        d_e = h.shape[1] // 2
        gate = jnp.minimum(h[:, :d_e], SWIGLU_LIMIT)
        up = jnp.clip(h[:, d_e:], -SWIGLU_LIMIT, SWIGLU_LIMIT)
        glu = gate * jax.nn.sigmoid(gate * SWIGLU_ALPHA)
        y = _dot(((up + 1.0) * glu).astype(BF16), wdn[0]) + bdn[0]
        ys_o[...] = _pack_bf16_pairs(y)


def _experts(block_expert, n_used, xs, wgu, bgu, wdn, bdn):
    rows, w = xs.shape
    nblk = rows // EXPERT_BLOCK
    d_in, d_gu = wgu.shape[1], wgu.shape[2]
    d_e, d_out = wdn.shape[1], wdn.shape[2]
    grid_spec = pltpu.PrefetchScalarGridSpec(
        num_scalar_prefetch=2,
        grid=(nblk,),
        in_specs=[
            pl.BlockSpec((EXPERT_BLOCK, w), lambda j, be, nu: (j, 0)),
            pl.BlockSpec((1, d_in, d_gu), lambda j, be, nu: (be[j], 0, 0)),
            pl.BlockSpec((1, 1, d_gu), lambda j, be, nu: (be[j], 0, 0)),
            pl.BlockSpec((1, d_e, d_out), lambda j, be, nu: (be[j], 0, 0)),
            pl.BlockSpec((1, 1, d_out), lambda j, be, nu: (be[j], 0, 0)),
        ],
        out_specs=pl.BlockSpec((EXPERT_BLOCK, d_out // 2), lambda j, be, nu: (j, 0)),
    )
    return pl.pallas_call(
        _expert_kernel,
        grid_spec=grid_spec,
        out_shape=jax.ShapeDtypeStruct((rows, d_out // 2), U32),
        compiler_params=pltpu.CompilerParams(dimension_semantics=("arbitrary",), vmem_limit_bytes=VMEM_LIMIT),
        name="experts",
    )(block_expert, n_used, xs, wgu, bgu, wdn, bdn)


def _final_kernel(resid_ref, gate_ref, dest_ref, ys_ref, lng, lnb, y_o, buf, sem):
    tg = resid_ref.shape[0]

    def body(i, c):
        tok = lax.shift_right_logical(i, TOP_K.bit_length() - 1)
        slot = jnp.bitwise_and(i, TOP_K - 1)
        pltpu.make_async_copy(ys_ref.at[pl.ds(dest_ref[i], 1)], buf.at[slot, pl.ds(tok, 1)], sem).start()
        return c

    lax.fori_loop(0, tg * TOP_K, body, 0)
    for k in range(TOP_K):
        pltpu.make_async_copy(ys_ref.at[pl.ds(0, tg)], buf.at[k], sem).wait()
    f_lo = jnp.zeros(buf.shape[1:], F32)
    f_hi = jnp.zeros(buf.shape[1:], F32)
    for k in range(TOP_K):
        lo, hi = _unpack_bf16_pairs(buf[k])
        gk = gate_ref[:, k:k + 1]
        f_lo = f_lo + gk * lo
        f_hi = f_hi + gk * hi
    ffn = jnp.concatenate([f_lo, f_hi], axis=1)
    y_o[...] = _layer_norm(resid_ref[...] + ffn, lng[...], lnb[...])


def _combine(resid, gate, dest, ys, lng, lnb):
    t = resid.shape[0]
    tg = min(TOKEN_TILE, t)
    row = lambda w: pl.BlockSpec((tg, w), lambda i: (i, 0))
    return pl.pallas_call(
        _final_kernel,
        grid=(t // tg,),
        in_specs=[row(D_MODEL), row(LANES),
                  pl.BlockSpec((tg * TOP_K,), lambda i: (i,), memory_space=pltpu.SMEM),
                  pl.BlockSpec(memory_space=pl.ANY),
                  _const_spec(lng.shape), _const_spec(lnb.shape)],
        out_specs=row(D_MODEL),
        out_shape=jax.ShapeDtypeStruct((t, D_MODEL), F32),
        scratch_shapes=[pltpu.VMEM((TOP_K, tg, ys.shape[1]), U32), pltpu.SemaphoreType.DMA(())],
        compiler_params=pltpu.CompilerParams(dimension_semantics=("arbitrary",), vmem_limit_bytes=VMEM_LIMIT),
        name="combine",
    )(resid, gate, dest, ys, lng, lnb)


def _layer(x_prompt, x_sample, cache_k, cache_v, cache_logf, p_prompt, p_sample,
           w_in, b_in, lnv_g, lnv_b, w_s, b_s, w_pa, w_pb, w_o, ln1_g, ln1_b,
           w_router, b_router, w_gu, b_gu, w_dn, b_dn, w_ple_gate, b_ple_gate, w_ple_proj, ln2_g, ln2_b,
           dn_alpha):
    nb, seq, d = x_prompt.shape
    sb, new, _ = x_sample.shape
    past = cache_k.shape[1]
    tp, ts = nb * seq, sb * new
    vec = lambda a: a.reshape(1, -1).astype(F32)

    ws, bs = _proj_weights(w_in, b_in)
    lng, lnb = vec(lnv_g), vec(lnv_b)
    xp = x_prompt.reshape(tp, d)
    xsm = x_sample.reshape(ts, d)
    u_p, va_p, k_p, v_p, lf_p, g_p, qa_p, ka_p, vt_p = _project_prompt(xp, seq, ws, bs, lng, lnb)
    u_s, va_s, k_s, v_s, lf_s, g_s, q_s = _project_sample(xsm, ws, bs, lng, lnb)

    attn_p = _attend_prompt(qa_p, ka_p, vt_p, nb, seq)
    cache_lf = jnp.pad(cache_logf.astype(F32), ((0, 0), (0, 0), (0, LANES - N_HEADS)))
    attn_s = _attend_sample(q_s.reshape(sb, new, d), k_s.reshape(sb, new, d), v_s.reshape(sb, new, d),
                            lf_s.reshape(sb, new, LANES), cache_k.reshape(sb, past, d),
                            cache_v.reshape(sb, past, d), cache_lf)

    wr = jnp.pad(w_router.astype(F32), ((0, 0), (0, LANES - N_EXPERTS)))
    wr_hi = wr.astype(BF16)
    wr_lo = (wr - wr_hi.astype(F32)).astype(BF16)
    br = jnp.concatenate([b_router.astype(F32), jnp.full((LANES - N_EXPERTS,), NEG_BIG, F32)]).reshape(1, LANES)
    spatial = b_s.shape[1]
    bst = jnp.pad(b_s.astype(F32).T, ((0, 0), (0, LANES - b_s.shape[0])))
    post_consts = [w_s.astype(F32), bst, w_pa.astype(BF16), w_pb.astype(BF16), w_o.astype(BF16),
                   vec(ln1_g), vec(ln1_b), wr_hi, wr_lo, br,
                   w_ple_gate.astype(BF16), vec(b_ple_gate), w_ple_proj.astype(BF16)]
    base0 = jnp.zeros((1, LANES), F32)
    x1p_p, resid_p, topi_p, gate_p, rank_p, cnt_p = _post_attention(
        xp, u_p, va_p, g_p, attn_p, p_prompt.reshape(tp, -1), base0, post_consts, min(spatial, seq), dn_alpha)
    x1p_s, resid_s, topi_s, gate_s, rank_s, cnt = _post_attention(
        xsm, u_s, va_s.astype(BF16), g_s, attn_s, p_sample.reshape(ts, -1), cnt_p, post_consts,
        min(spatial, new), dn_alpha)

    counts = cnt[0, :N_EXPERTS].astype(I32)
    padded = (counts + EXPERT_BLOCK - 1) // EXPERT_BLOCK * EXPERT_BLOCK
    pad_end = jnp.cumsum(padded)
    pad_start = (pad_end - padded).astype(I32)
    n_assign = (tp + ts) * TOP_K
    nblk = -(-n_assign // EXPERT_BLOCK) + N_EXPERTS
    n_rows = nblk * EXPERT_BLOCK
    block_expert = jnp.minimum(
        jnp.searchsorted(pad_end, jnp.arange(nblk, dtype=I32) * EXPERT_BLOCK, side="right"),
        N_EXPERTS - 1).astype(I32)
    n_used = (pad_end[-1:] // EXPERT_BLOCK).astype(I32)

    dest_of = lambda topi, rank: (jnp.take(pad_start, topi[:, :TOP_K]) + rank[:, :TOP_K]).reshape(-1)
    dest_p, dest_s = dest_of(topi_p, rank_p), dest_of(topi_s, rank_s)
    xs = jnp.zeros((n_rows, d // 2), U32)
    xs = _dispatch(x1p_p, dest_p, xs)
    xs = _dispatch(x1p_s, dest_s, xs)

    ys = _experts(block_expert, n_used, xs, w_gu.astype(BF16), b_gu.astype(F32)[:, None, :],
                  w_dn.astype(BF16), b_dn.astype(F32)[:, None, :])

    l2g, l2b = vec(ln2_g), vec(ln2_b)
    y_p = _combine(resid_p, gate_p, dest_p, ys, l2g, l2b)
    y_s = _combine(resid_s, gate_s, dest_s, ys, l2g, l2b)

    shp = lambda a, b_, n: a.reshape(b_, n, N_HEADS, HEAD_DIM)
    return (y_p.reshape(nb, seq, d), y_s.reshape(sb, new, d),
            shp(k_p, nb, seq), shp(v_p, nb, seq), lf_p.reshape(nb, seq, N_HEADS),
            shp(k_s, sb, new), shp(v_s, sb, new), lf_s[:, :N_HEADS].reshape(sb, new, N_HEADS),
            va_s.reshape(sb, new, N_GROUPS, d // N_GROUPS))


def kernel(x_prompt, x_sample, cache_fox_k, cache_fox_v, cache_fox_logf, p_prompt, p_sample, w_in, b_in, lnv_g, lnv_b, w_s, b_s, w_pa, w_pb, w_o, ln1_g, ln1_b, w_router, b_router, w_gu, b_gu, w_dn, b_dn, w_ple_gate, b_ple_gate, w_ple_proj, ln2_g, ln2_b):
    depth = w_in.shape[0]
    assert depth == 1, "the layer loop below carries one layer"
    dn_alpha = float((2 * depth) ** 0.25)
    outs = _layer(x_prompt, x_sample, cache_fox_k[0], cache_fox_v[0], cache_fox_logf[0], p_prompt[0], p_sample[0],
                  w_in[0], b_in[0], lnv_g[0], lnv_b[0], w_s[0], b_s[0], w_pa[0], w_pb[0], w_o[0], ln1_g[0], ln1_b[0],
                  w_router[0], b_router[0], w_gu[0], b_gu[0], w_dn[0], b_dn[0], w_ple_gate[0], b_ple_gate[0],
                  w_ple_proj[0], ln2_g[0], ln2_b[0], dn_alpha)
    y_p, y_s, k_p, v_p, lf_p, k_s, v_s, lf_s, va_s = outs
    lead = lambda a: a[None]
    return (y_p, y_s, lead(k_p), lead(v_p), lead(lf_p), lead(k_s), lead(v_s), lead(lf_s), lead(va_s))
```

```python
import functools

import jax
import jax.numpy as jnp
import numpy as np
from jax import lax
from jax.experimental import pallas as pl
from jax.experimental.pallas import tpu as pltpu

F32 = jnp.float32
BF16 = jnp.bfloat16
I32 = jnp.int32

LANES = 128
SUBLANES = 8
D_MODEL = 1024
N_HEADS = 16
HEAD_DIM = 64
N_GROUPS = 8
N_EXPERTS = 32
TOP_K = 4
SWIGLU_ALPHA = 1.702
SWIGLU_LIMIT = 7.0
LN_EPS = 1e-5
NEG_BIG = -1e30
LOG2_E = 1.4426950408889634
DEN_ROWS = 16

TOKEN_TILE = 256
Q_TILE = 512
EXPERT_BLOCK = 256
DISPATCH_TILE = 1024
COMBINE_TILE = 256
DMA_UNROLL = 8
VMEM_LIMIT = 56 * 1024 * 1024

F_MID_SHIFT = N_HEADS
F_LO_SHIFT = 2 * N_HEADS
F_ONE_LANE = 3 * N_HEADS


def _dot(a, b):
    return jnp.dot(a, b, preferred_element_type=F32)


def _dot_nt(a, b):
    return lax.dot_general(a, b, (((1,), (1,)), ((), ())), preferred_element_type=F32)


def _layer_norm(x, g, b):
    mu = jnp.mean(x, axis=-1, keepdims=True)
    xc = x - mu
    var = jnp.mean(xc * xc, axis=-1, keepdims=True)
    return xc * lax.rsqrt(var + LN_EPS) * g + b


def _log_sigmoid(z):
    return jnp.minimum(z, 0.0) - jnp.log1p(jnp.exp(-jnp.abs(z)))


def _split3(x):
    hi = x.astype(BF16).astype(F32)
    r = x - hi
    mid = r.astype(BF16).astype(F32)
    lo = (r - mid).astype(BF16).astype(F32)
    return hi, mid, lo


def _tri(n, strict):
    r = lax.broadcasted_iota(I32, (n, n), 0)
    c = lax.broadcasted_iota(I32, (n, n), 1)
    keep = (r > c) if strict else (r >= c)
    return jnp.where(keep, 1.0, 0.0).astype(F32)


def _cumsum_rows(x):
    n = x.shape[0]
    hi, mid, lo = _split3(x)
    cat = jnp.concatenate([hi, mid, lo], axis=1).astype(BF16)
    c = _dot(_tri(n, strict=False).astype(BF16), cat)
    return (c[:, 2 * LANES:] + c[:, LANES:2 * LANES]) + c[:, :LANES]


def _proj_sections(x_ref, wuv, wkv, wf, wg, buv, bkv, bf_, bg, lng, lnb, u_o, k_o, v_o, g_o):
    xb = x_ref[...].astype(BF16)
    pu = _dot(xb, wuv[:, :D_MODEL]) + buv[:, :D_MODEL]
    u_o[...] = jax.nn.gelu(pu).astype(BF16)
    pv = _dot(xb, wuv[:, D_MODEL:]) + buv[:, D_MODEL:]
    va = _layer_norm(jax.nn.gelu(pv), lng[...], lnb[...])
    k = _dot(xb, wkv[:, :D_MODEL]) + bkv[:, :D_MODEL]
    k_o[...] = k.reshape(k_o.shape)
    v =_dot(xb, wkv[:, D_MODEL:]) + bkv[:, D_MODEL:]
    v_o[...] = v.reshape(v_o.shape)
    g_o[...] = jax.nn.sigmoid(_dot(xb, wg[...]) + bg[...]).astype(BF16)
    f = _dot(xb, wf[...]) + bf_[...]
    lane = lax.broadcasted_iota(I32, f.shape, 1)
    lf = jnp.where(lane < N_HEADS, _log_sigmoid(f), 0.0)
    return xb, va, k, v, lf


def _proj_prompt_kernel(x_ref, wuv, wq, wkv, wf, wg, buv, bq, bkv, bf_, bg, lng, lnb, eq, ek,
                        u_o, va_o, k_o, v_o, lf_o, g_o, qa_o, ka_o, vt_o, carry, *, tiles_per_seq):
    i = pl.program_id(0)
    xb, va, k, v, lf = _proj_sections(x_ref, wuv, wkv, wf, wg, buv, bkv, bf_, bg, lng, lnb, u_o, k_o, v_o, g_o)
    va_o[...] = va.astype(BF16)
    vt_o[0, 0] = v.T.astype(BF16)
    lf_o[...] = lf[:, :N_HEADS]

    @pl.when(i % tiles_per_seq == 0)
    def _():
        carry[...] = jnp.zeros_like(carry)

    tm = lf.shape[0]
    fcum = carry[...] + _cumsum_rows(lf)
    carry[...] = fcum[tm - 1:tm, :]

    hi, mid, lo = _split3(fcum * LOG2_E)
    lane = lax.broadcasted_iota(I32, fcum.shape, 1)
    pack = (hi + pltpu.roll(mid, F_MID_SHIFT, 1) + pltpu.roll(lo, F_LO_SHIFT, 1)
            + jnp.where(lane == F_ONE_LANE, 1.0, 0.0)).astype(BF16)
    add_q = _dot(pack, eq[...])
    add_k = _dot(pack, ek[...])
    q = (_dot(xb, wq[...]) + bq[...]) * (HEAD_DIM ** -0.5 * LOG2_E)
    low =lane < HEAD_DIM
    for g in range(N_GROUPS):
        sl = slice(g * LANES, (g + 1) * LANES)
        ev = slice(2 * g * LANES, (2 * g + 1) * LANES)
        od = slice((2 * g + 1) * LANES, (2 * g + 2) * LANES)
        qa_o[:, ev] = jnp.where(low, q[:, sl], add_q[:, ev]).astype(BF16)
        qa_o[:, od] = jnp.where(low, add_q[:, od], q[:, sl]).astype(BF16)
        ka_o[:, ev] = jnp.where(low, k[:, sl], add_k[:, ev]).astype(BF16)
        ka_o[:, od] = jnp.where(low, add_k[:, od], k[:, sl]).astype(BF16)


def _proj_sample_kernel(x_ref, wuv, wq, wkv, wf, wg, buv, bq, bkv, bf_, bg, lng, lnb,
                        u_o, va_o, k_o, v_o, lf_o, g_o, q_o):
    xb, va, _, _, lf = _proj_sections(x_ref, wuv, wkv, wf, wg, buv, bkv, bf_, bg, lng, lnb, u_o, k_o, v_o, g_o)
    va_o[...] = va
    lf_o[...] = lf
    q_o[...] = (_dot(xb, wq[...]) + bq[...]) * (HEAD_DIM ** -0.5)


def _const_spec(shape):
    nd = len(shape)
    return pl.BlockSpec(shape, lambda *_: (0,) * nd, pipeline_mode=pl.Buffered(1))


def _proj_weights(w_in, b_in):
    d = D_MODEL
    off_q, off_k, off_f = 2 * d, 3 * d, 5 * d
    off_ga = off_f + N_HEADS
    wuv = w_in[:, :off_q].astype(BF16)
    wq = w_in[:, off_q:off_k].astype(BF16)
    wkv = w_in[:, off_k:off_f].astype(BF16)
    wf = jnp.pad(w_in[:, off_f:off_ga], ((0, 0), (0, LANES - N_HEADS))).astype(BF16)
    wg = w_in[:, off_ga:].astype(BF16)
    b = b_in.reshape(1, -1).astype(F32)
    buv, bq, bkv = b[:, :off_q], b[:, off_q:off_k], b[:, off_k:off_f]
    bf_ = jnp.pad(b[:, off_f:off_ga], ((0, 0), (0, LANES - N_HEADS)))
    bg = b[:, off_ga:]
    return (wuv, wq, wkv, wf, wg), (buv, bq, bkv, bf_, bg)


def _spread_matrices():
    eq_np = np.zeros((LANES, 2 * D_MODEL), np.float32)
    ek_np = np.zeros((LANES, 2 * D_MODEL), np.float32)
    for h in range(N_HEADS):
        base = LANES * h + (HEAD_DIM if h % 2 == 0 else 0)
        for c, shift in enumerate((0, F_MID_SHIFT, F_LO_SHIFT)):
            eq_np[shift + h, base + c] = 1.0
            eq_np[F_ONE_LANE, base + 3 + c] = 1.0
            ek_np[F_ONE_LANE, base + c] = 1.0
            ek_np[shift + h, base + 3 + c] = -1.0
    return jnp.asarray(eq_np, BF16), jnp.asarray(ek_np, BF16)


def _project_prompt(x2d, seq, ws, bs, lng, lnb):
    t = x2d.shape[0]
    tm = min(TOKEN_TILE, seq)
    nt = t // tm
    nb = t // seq
    eq, ek = _spread_matrices()
    consts = list(ws) + list(bs) + [lng, lnb, eq, ek]
    row = lambda w: pl.BlockSpec((tm, w), lambda i: (i, 0))
    out_shape = (
        jax.ShapeDtypeStruct((t, D_MODEL), BF16),
        jax.ShapeDtypeStruct((t, D_MODEL), BF16),
        jax.ShapeDtypeStruct((t, N_HEADS, HEAD_DIM), F32),
        jax.ShapeDtypeStruct((t, N_HEADS, HEAD_DIM), F32),
        jax.ShapeDtypeStruct((t, N_HEADS), F32),
        jax.ShapeDtypeStruct((t, 2 * D_MODEL), BF16),
        jax.ShapeDtypeStruct((t, 2 * D_MODEL), BF16),
        jax.ShapeDtypeStruct((t, 2 * D_MODEL), BF16),
        jax.ShapeDtypeStruct((nb, seq // tm, D_MODEL, tm), BF16),
    )
    tiles_per_seq = seq // tm
    heads = pl.BlockSpec((tm, N_HEADS, HEAD_DIM), lambda i: (i, 0, 0))
    out_specs = (row(D_MODEL), row(D_MODEL), heads, heads, row(N_HEADS),
                 row(2 * D_MODEL), row(2 * D_MODEL), row(2 * D_MODEL),
                 pl.BlockSpec((1, 1, D_MODEL, tm), lambda i: (i // tiles_per_seq, i % tiles_per_seq, 0, 0)))
    return pl.pallas_call(
        functools.partial(_proj_prompt_kernel, tiles_per_seq=tiles_per_seq),
        grid=(nt,),
        in_specs=[row(D_MODEL)] + [_const_spec(c.shape) for c in consts],
        out_specs=out_specs,
        out_shape=out_shape,
        scratch_shapes=[pltpu.VMEM((1, LANES), F32)],
        compiler_params=pltpu.CompilerParams(dimension_semantics=("arbitrary",), vmem_limit_bytes=VMEM_LIMIT),
        name="proj_prompt",
    )(x2d, *consts)


def _project_sample(x2d, ws, bs, lng, lnb):
    t = x2d.shape[0]
    tm = min(TOKEN_TILE, t)
    nt = t // tm
    consts = list(ws) + list(bs) + [lng, lnb]
    row = lambda w: pl.BlockSpec((tm, w), lambda i: (i, 0))
    out_shape = (
        jax.ShapeDtypeStruct((t, D_MODEL), BF16),
        jax.ShapeDtypeStruct((t, D_MODEL), F32),
        jax.ShapeDtypeStruct((t, D_MODEL), F32),
        jax.ShapeDtypeStruct((t, D_MODEL), F32),
        jax.ShapeDtypeStruct((t, LANES), F32),
        jax.ShapeDtypeStruct((t, 2 * D_MODEL), BF16),
        jax.ShapeDtypeStruct((t, D_MODEL), F32),
    )
    out_specs = (row(D_MODEL), row(D_MODEL), row(D_MODEL), row(D_MODEL), row(LANES), row(2 * D_MODEL), row(D_MODEL))
    return pl.pallas_call(
        _proj_sample_kernel,
        grid=(nt,),
        in_specs=[row(D_MODEL)] + [_const_spec(c.shape) for c in consts],
        out_specs=out_specs,
        out_shape=out_shape,
        compiler_params=pltpu.CompilerParams(dimension_semantics=("arbitrary",), vmem_limit_bytes=VMEM_LIMIT),
        name="proj_sample",
    )(x2d, *consts)


def _attn_prompt_kernel(q_ref, k_ref, vt_ref, o_ref, m_ref, acc_ref, ot_ref,
                        s0, s1, p0, p1, a0, a1, c0, c1, *, tq, tk):
    qi = pl.program_id(2)
    assert tq == 2 * tk, "the pipeline below is written for two diagonal spans per query tile"
    m_ref[...] = jnp.full(m_ref.shape, NEG_BIG, F32)
    acc_ref[...] = jnp.zeros(acc_ref.shape, F32)
    p1[...] = jnp.zeros(p1.shape, BF16)
    a1[...] = jnp.ones(a1.shape, F32)
    ones = jnp.ones((DEN_ROWS, tk), BF16)
    lanes = lambda hh: slice(hh * LANES, (hh + 1) * LANES)

    def score(t, s_slot, c_slot):
        start = pl.multiple_of(t * tk, tk)
        for hh in range(2):
            st = _dot_nt(k_ref[pl.ds(start, tk), lanes(hh)], q_ref[:, lanes(hh)])
            s_slot[hh] = st
            c_slot[hh] = jnp.max(st, axis=0, keepdims=True)

    def softmax(s_slot, c_slot, p_slot, a_slot, mask_offset):
        for hh in range(2):
            st = s_slot[hh]
            if mask_offset is None:
                col_max = c_slot[hh]
            else:
                key_row = lax.broadcasted_iota(I32, (tk, tq), 0)
                qry_col = lax.broadcasted_iota(I32, (tk, tq), 1)
                st = jnp.where(key_row + mask_offset <= qry_col, st, NEG_BIG)
                col_max = jnp.max(st, axis=0, keepdims=True)
            m_old = m_ref[hh]
            m_new = jnp.maximum(m_old, col_max)
            a_slot[hh] = jnp.exp2(m_old - m_new)
            p_slot[hh] = jnp.exp2((st - m_new).astype(BF16))
            m_ref[hh] = m_new

    def value(t, p_slot, a_slot):
        tile = jnp.maximum(t, 0)
        for hh in range(2):
            vt = vt_ref[0, tile, hh * HEAD_DIM:(hh + 1) * HEAD_DIM, :]
            acc_ref[hh] = a_slot[hh] * acc_ref[hh] + _dot(jnp.concatenate([vt, ones], axis=0), p_slot[hh])

    score(0, s0, c0)

    def two_steps(ii):
        t = 2 * ii + 1
        value(t - 2, p1, a1)
        score(t, s1, c1)
        softmax(s0, c0, p0, a0, None)
        value(t - 1, p0, a0)
        score(t + 1, s0, c0)
        softmax(s1, c1, p1, a1, None)

    def body(jj, c):
        two_steps(2 * jj)
        two_steps(2 * jj + 1)
        return c

    lax.fori_loop(0, qi // 2, body, 0)

    @pl.when(qi % 2 == 1)
    def _():
        two_steps(qi - 1)

    t = 2 * qi + 1
    value(t - 2, p1, a1)
    score(t, s1, c1)
    softmax(s0, c0, p0, a0, 0)
    value(t - 1, p0, a0)
    softmax(s1, c1, p1, a1, tk)
    value(t, p1, a1)
    for hh in range(2):
        ot_ref[hh * HEAD_DIM:(hh + 1) * HEAD_DIM, :] = acc_ref[hh, :HEAD_DIM] / acc_ref[hh, HEAD_DIM:HEAD_DIM + 1]
    o_ref[...] = ot_ref[...].T.astype(BF16)


def _attend_prompt(q_aug, k_aug, v_t, nb, seq):
    t = q_aug.shape[0]
    tk = v_t.shape[3]
    tq = min(Q_TILE, seq)
    nq = seq // tq
    return pl.pallas_call(
        functools.partial(_attn_prompt_kernel, tq=tq, tk=tk),
        grid=(nb, N_GROUPS, nq),
        in_specs=[
            pl.BlockSpec((tq, 2 * LANES), lambda b, g, i: (b * nq + i, g)),
            pl.BlockSpec((seq, 2 * LANES), lambda b, g, i: (b, g)),
            pl.BlockSpec((1, seq // tk, LANES, tk), lambda b, g, i: (b, 0, g, 0)),
        ],
        out_specs=pl.BlockSpec((tq, LANES), lambda b, g, i: (b * nq + i, g)),
        out_shape=jax.ShapeDtypeStruct((t, D_MODEL), BF16),
        scratch_shapes=[pltpu.VMEM((2, 1, tq), F32), pltpu.VMEM((2, HEAD_DIM + DEN_ROWS, tq), F32),
                        pltpu.VMEM((LANES, tq), F32),
                        pltpu.VMEM((2, tk, tq), F32), pltpu.VMEM((2, tk, tq), F32),
                        pltpu.VMEM((2, tk, tq), BF16), pltpu.VMEM((2, tk, tq), BF16),
                        pltpu.VMEM((2, 1, tq), F32), pltpu.VMEM((2, 1, tq), F32),
                        pltpu.VMEM((2, 1, tq), F32), pltpu.VMEM((2, 1, tq), F32)],
        compiler_params=pltpu.CompilerParams(
            dimension_semantics=("arbitrary", "arbitrary", "arbitrary"), vmem_limit_bytes=VMEM_LIMIT),
        name="attn_prompt",
    )(q_aug, k_aug, v_t)


def _attn_sample_kernel(q_ref, k_ref, v_ref, lf_ref, ck_ref, cv_ref, clf_ref, o_ref, *, chunk):
    past = ck_ref.shape[1]
    new = q_ref.shape[1]
    clf = clf_ref[0]
    carry = jnp.zeros((1, LANES), F32)
    parts = []
    for c in range(past // chunk):
        fc = carry + _cumsum_rows(clf[c * chunk:(c + 1) * chunk])
        carry = fc[chunk - 1:chunk, :]
        parts.append(fc)
    f_cache = jnp.concatenate(parts, axis=0)
    f_new = carry + _cumsum_rows(lf_ref[0])
    f_cache_t = f_cache.T
    f_new_t = f_new.T
    lane = lax.broadcasted_iota(I32, (new, LANES), 1)
    causal = lax.broadcasted_iota(I32, (new, new), 1) <= lax.broadcasted_iota(I32, (new, new), 0)
    for g in range(N_GROUPS):
        sl = slice(g * LANES, (g + 1) * LANES)
        qg = q_ref[0, :, sl]
        kc = ck_ref[0, :, sl].astype(BF16)
        vc = cv_ref[0, :, sl].astype(BF16)
        kn = k_ref[0, :, sl].astype(BF16)
        vn = v_ref[0, :, sl].astype(BF16)
        outs = []
        for hh in range(2):
            h = 2 * g + hh
            mine = (lane < HEAD_DIM) if hh == 0 else (lane >= HEAD_DIM)
            qm = jnp.where(mine, qg, 0.0).astype(BF16)
            fq = f_new[:, h:h + 1]
            s_c = _dot_nt(qm, kc) + fq - f_cache_t[h:h + 1, :]
            s_n = _dot_nt(qm, kn) + fq - f_new_t[h:h + 1, :]
            s_n = jnp.where(causal, s_n, NEG_BIG)
            m = jnp.maximum(jnp.max(s_c, axis=1, keepdims=True), jnp.max(s_n, axis=1, keepdims=True))
            p_c = jnp.exp(s_c - m)
            p_n = jnp.exp(s_n - m)
            den = jnp.sum(p_c, axis=1, keepdims=True) + jnp.sum(p_n, axis=1, keepdims=True)
            outs.append((_dot(p_c.astype(BF16), vc) + _dot(p_n.astype(BF16), vn)) / den)
        o_ref[:, sl] = jnp.where(lane < HEAD_DIM, outs[0], outs[1]).astype(BF16)


def _attend_sample(q, k, v, lf, cache_k, cache_v, cache_lf):
    nb, new, _ = q.shape
    past = cache_k.shape[1]
    chunk = min(TOKEN_TILE, past)
    blk = lambda n, w: pl.BlockSpec((1, n, w), lambda b: (b, 0, 0))
    return pl.pallas_call(
        functools.partial(_attn_sample_kernel, chunk=chunk),
        grid=(nb,),
        in_specs=[blk(new, D_MODEL), blk(new, D_MODEL), blk(new, D_MODEL), blk(new, LANES),
                  blk(past, D_MODEL), blk(past, D_MODEL), blk(past, LANES)],
        out_specs=pl.BlockSpec((new, D_MODEL), lambda b: (b, 0)),
        out_shape=jax.ShapeDtypeStruct((nb * new, D_MODEL), BF16),
        compiler_params=pltpu.CompilerParams(dimension_semantics=("arbitrary",), vmem_limit_bytes=VMEM_LIMIT),
        name="attn_sample",
    )(q, k, v, lf, cache_k, cache_v, cache_lf)


def _post_kernel(x_ref, u_ref, va_ref, g_ref, at_ref, p_ref, base_ref, ws_ref, bst_ref, wpa, wpb, wo,
                 l1g, l1b, wr_hi, wr_lo, br, wpg, bpg, wpp,
                 x1_o, resid_o, topi_o, gate_o, rank_o, cnt_o, a_sc, base_sc, *, chunk, dn_alpha):
    i = pl.program_id(0)
    tm = x_ref.shape[0]

    @pl.when(i == 0)
    def _():
        base_sc[...] = base_ref[...]

    tri = _tri(chunk, strict=False)
    for g in range(N_GROUPS):
        sl = slice(g * LANES, (g + 1) * LANES)
        wm = (ws_ref[g, :chunk, :chunk] * tri).astype(BF16)
        bias = bst_ref[:chunk, g:g + 1]
        for c in range(tm // chunk):
            rows = slice(c * chunk, (c + 1) * chunk)
            sg = _dot(wm, va_ref[rows, sl]) + bias
            a_sc[rows, sl] = (u_ref[rows, sl].astype(F32) * sg).astype(BF16)

    pa = _dot(a_sc[...], wpa[...])
    pb = _dot(at_ref[...], wpb[...])
    merged = g_ref[:, :D_MODEL].astype(F32) * pa + g_ref[:, D_MODEL:].astype(F32) * pb
    x1 = _layer_norm(dn_alpha * x_ref[...] + _dot(merged.astype(BF16), wo[...]), l1g[...], l1b[...])
    x1b = x1.astype(BF16)
    x1_o[...] = x1

    ple = jax.nn.sigmoid(_dot(x1b, wpg[...]) + bpg[...]) * _dot(p_ref[...].astype(BF16), wpp[...])
    resid_o[...] = dn_alpha * x1 + ple

    x1_lo = (x1 - x1b.astype(F32)).astype(BF16)
    logits = (_dot(x1b, wr_hi[...]) + (_dot(x1b, wr_lo[...]) + _dot(x1_lo, wr_hi[...]))) + br[...]
    lane = lax.broadcasted_iota(I32, (tm, LANES), 1)
    work = logits
    vals, idxs = [], []
    for _ in range(TOP_K):
        mk = jnp.max(work, axis=1, keepdims=True)
        ik = jnp.min(jnp.where(work == mk, lane, LANES), axis=1, keepdims=True)
        vals.append(mk)
        idxs.append(ik)
        work = jnp.where(lane == ik, -jnp.inf, work)
    exps = [jnp.exp(v - vals[0]) for v in vals]
    den = exps[0] + exps[1] + exps[2] + exps[3]

    lower = _tri(tm, strict=True).astype(BF16)
    run = base_sc[...]
    topi = jnp.zeros((tm, LANES), I32)
    gate = jnp.zeros((tm, LANES), F32)
    rank = jnp.zeros((tm, LANES), F32)
    for k in range(TOP_K):
        onehot = jnp.where(lane == idxs[k], 1.0, 0.0)
        before = _dot(lower, onehot.astype(BF16))
        rk = jnp.sum(onehot * (before + run), axis=1, keepdims=True)
        topi = jnp.where(lane == k, idxs[k], topi)
        gate = jnp.where(lane == k, exps[k] / den, gate)
        rank = jnp.where(lane == k, rk, rank)
        run = run + jnp.sum(onehot, axis=0, keepdims=True)
    base_sc[...] = run
    topi_o[...] = topi
    gate_o[...] = gate
    rank_o[...] = rank.astype(I32)
    cnt_o[...] = run


def _post_attention(x2d, u, va, gates, attn, p2d, base, consts, chunk, dn_alpha):
    t = x2d.shape[0]
    tm = min(TOKEN_TILE, t)
    nt = t // tm
    row = lambda w: pl.BlockSpec((tm, w), lambda i: (i, 0))
    ple_dim = p2d.shape[1]
    out_shape = (
        jax.ShapeDtypeStruct((t, D_MODEL), F32),
        jax.ShapeDtypeStruct((t, D_MODEL), F32),
        jax.ShapeDtypeStruct((t, LANES), I32),
        jax.ShapeDtypeStruct((t, LANES), F32),
        jax.ShapeDtypeStruct((t, LANES), I32),
        jax.ShapeDtypeStruct((1, LANES), F32),
    )
    out_specs = (row(D_MODEL), row(D_MODEL), row(LANES), row(LANES), row(LANES),
                 pl.BlockSpec((1, LANES), lambda i: (0, 0)))
    return pl.pallas_call(
        functools.partial(_post_kernel, chunk=chunk, dn_alpha=dn_alpha),
        grid=(nt,),
        in_specs=[row(D_MODEL), row(D_MODEL), row(D_MODEL), row(2 * D_MODEL), row(D_MODEL), row(ple_dim),
                  _const_spec(base.shape)] + [_const_spec(c.shape) for c in consts],
        out_specs=out_specs,
        out_shape=out_shape,
        scratch_shapes=[pltpu.VMEM((tm, D_MODEL), BF16), pltpu.VMEM((1, LANES), F32)],
        compiler_params=pltpu.CompilerParams(dimension_semantics=("arbitrary",), vmem_limit_bytes=VMEM_LIMIT),
        name="post_attention",
    )(x2d, u, va, gates, attn, p2d, base, *consts)


def _dispatch_kernel(fill_start_ref, fill_len_ref, nu_ref, xp_ref, xn_ref, dest_ref, xs_out, zero_sc, sem, zsem,
                     *, prompt_steps, n_blocks):
    i = pl.program_id(0)
    td = xp_ref.shape[0]

    @pl.when(i == 0)
    def _():
        zero_sc[...] = jnp.zeros(zero_sc.shape, zero_sc.dtype)

        def zero_fill(wait):
            def go(copy):
                copy.wait() if wait else copy.start()

            def per_expert(e, c):
                n = fill_len_ref[e]
                base = fill_start_ref[e]
                head = jnp.bitwise_and(n, SUBLANES - 1)
                for r in range(SUBLANES - 1):
                    @pl.when(r < head)
                    def _(r=r):
                        go(pltpu.make_async_copy(zero_sc.at[pl.ds(0, 1)], xs_out.at[pl.ds(base + r, 1)], zsem))

                piece = EXPERT_BLOCK // 2
                while piece >= SUBLANES:
                    offset = pl.multiple_of(base + head + jnp.bitwise_and(n, -(2 * piece)), SUBLANES)

                    @pl.when(jnp.bitwise_and(n, piece) != 0)
                    def _(piece=piece, offset=offset):
                        go(pltpu.make_async_copy(zero_sc.at[pl.ds(0, piece)], xs_out.at[pl.ds(offset, piece)], zsem))

                    piece //= 2
                return c

            lax.fori_loop(0, N_EXPERTS, per_expert, 0)

            def per_block(j, c):
                row0 = pl.multiple_of(j * EXPERT_BLOCK, EXPERT_BLOCK)
                go(pltpu.make_async_copy(zero_sc, xs_out.at[pl.ds(row0, EXPERT_BLOCK)], zsem))
                return c

            lax.fori_loop(nu_ref[0], n_blocks, per_block, 0)

        zero_fill(wait=False)
        zero_fill(wait=True)

    def scatter(x_ref):
        def body(j, c):
            for u in range(DMA_UNROLL):
                tok = j * (DMA_UNROLL // TOP_K) + u // TOP_K
                pltpu.make_async_copy(x_ref.at[pl.ds(tok, 1)], xs_out.at[pl.ds(dest_ref[j * DMA_UNROLL + u], 1)],
                                      sem).start(priority=u % 2)
            return c

        lax.fori_loop(0, td * TOP_K // DMA_UNROLL, body, 0)
        for _ in range(TOP_K):
            pltpu.make_async_copy(x_ref, xs_out.at[pl.ds(0, td)], sem).wait()

    @pl.when(i < prompt_steps)
    def _():
        scatter(xp_ref)

    @pl.when(i >= prompt_steps)
    def _():
        scatter(xn_ref)


def _dispatch(fill_start, fill_len, n_used, x1_prompt, x1_sample, dest, n_rows):
    tp, w = x1_prompt.shape
    ts = x1_sample.shape[0]
    td = min(DISPATCH_TILE, ts)
    prompt_steps, sample_steps = tp // td, ts // td
    grid_spec = pltpu.PrefetchScalarGridSpec(
        num_scalar_prefetch=3,
        grid=(prompt_steps + sample_steps,),
        in_specs=[pl.BlockSpec((td, w), lambda i, *_: (jnp.minimum(i, prompt_steps - 1), 0)),
                  pl.BlockSpec((td, w), lambda i, *_: (jnp.maximum(i - prompt_steps, 0), 0)),
                  pl.BlockSpec((td * TOP_K,), lambda i, *_: (i,), memory_space=pltpu.SMEM)],
        out_specs=pl.BlockSpec(memory_space=pl.ANY),
        scratch_shapes=[pltpu.VMEM((EXPERT_BLOCK, w), x1_prompt.dtype),
                        pltpu.SemaphoreType.DMA(()), pltpu.SemaphoreType.DMA(())],
    )
    return pl.pallas_call(
        functools.partial(_dispatch_kernel, prompt_steps=prompt_steps, n_blocks=n_rows // EXPERT_BLOCK),
        grid_spec=grid_spec,
        out_shape=jax.ShapeDtypeStruct((n_rows, w), x1_prompt.dtype),
        compiler_params=pltpu.CompilerParams(dimension_semantics=("arbitrary",), vmem_limit_bytes=VMEM_LIMIT),
        name="dispatch",
    )(fill_start, fill_len, n_used, x1_prompt, x1_sample, dest)


def _expert_kernel(be_ref, nu_ref, xs_ref, wgu, bgu, wdn, bdn, ys_o, wgu_bf, wdn_bf):
    j = pl.program_id(0)
    used = j < nu_ref[0]
    new_expert = jnp.logical_or(j == 0, be_ref[j] != be_ref[jnp.maximum(j - 1, 0)])

    @pl.when(jnp.logical_not(used))
    def _():
        ys_o[...] = jnp.zeros(ys_o.shape, ys_o.dtype)

    @pl.when(jnp.logical_and(used, new_expert))
    def _():
        wgu_bf[...] = wgu[0].astype(BF16)
        wdn_bf[...] = wdn[0].astype(BF16)

    @pl.when(used)
    def _():
        h = _dot(xs_ref[...].astype(BF16), wgu_bf[...]) + bgu[0]
        d_e = h.shape[1] // 2
        gate = jnp.minimum(h[:, :d_e], SWIGLU_LIMIT)
        up = jnp.clip(h[:, d_e:], -SWIGLU_LIMIT, SWIGLU_LIMIT)
        glu = gate * jax.nn.sigmoid(gate * SWIGLU_ALPHA)
        ys_o[...] = _dot(((up + 1.0) * glu).astype(BF16), wdn_bf[...]) + bdn[0]


def _experts(block_expert, n_used, xs, wgu, bgu, wdn, bdn):
    rows, w = xs.shape
    nblk = rows // EXPERT_BLOCK
    d_in, d_gu = wgu.shape[1], wgu.shape[2]
    d_e, d_out = wdn.shape[1], wdn.shape[2]
    grid_spec = pltpu.PrefetchScalarGridSpec(
        num_scalar_prefetch=2,
        grid=(nblk,),
        in_specs=[
            pl.BlockSpec((EXPERT_BLOCK, w), lambda j, be, nu: (j, 0)),
            pl.BlockSpec((1, d_in, d_gu), lambda j, be, nu: (be[j], 0, 0)),
            pl.BlockSpec((1, 1, d_gu), lambda j, be, nu: (be[j], 0, 0)),
            pl.BlockSpec((1, d_e, d_out), lambda j, be, nu: (be[j], 0, 0)),
            pl.BlockSpec((1, 1, d_out), lambda j, be, nu: (be[j], 0, 0)),
        ],
        out_specs=pl.BlockSpec((EXPERT_BLOCK, d_out), lambda j, be, nu: (j, 0)),
        scratch_shapes=[pltpu.VMEM((d_in, d_gu), BF16), pltpu.VMEM((d_e, d_out), BF16)],
    )
    return pl.pallas_call(
        _expert_kernel,
        grid_spec=grid_spec,
        out_shape=jax.ShapeDtypeStruct((rows, d_out), F32),
        compiler_params=pltpu.CompilerParams(dimension_semantics=("arbitrary",), vmem_limit_bytes=VMEM_LIMIT),
        name="experts",
    )(block_expert, n_used, xs, wgu, bgu, wdn, bdn)


def _final_kernel(resid_ref, gate_ref, dest_ref, dest_next_ref, ys_ref, lng, lnb, y_o, buf, sem, *, n):
    i = pl.program_id(0)
    tg = resid_ref.shape[0]

    def gather(idx_ref, slot):
        def body(j, c):
            for u in range(DMA_UNROLL):
                tok = j * (DMA_UNROLL // TOP_K) + u // TOP_K
                pltpu.make_async_copy(ys_ref.at[pl.ds(idx_ref[j * DMA_UNROLL + u], 1)],
                                      buf.at[slot, u % TOP_K, pl.ds(tok, 1)], sem.at[slot]).start(priority=u % 2)
            return c

        lax.fori_loop(0, tg * TOP_K // DMA_UNROLL, body, 0)

    @pl.when(i == 0)
    def _():
        gather(dest_ref, 0)

    @pl.when(i + 1 < n)
    def _():
        gather(dest_next_ref, (i + 1) % 2)

    slot = i % 2
    for k in range(TOP_K):
        pltpu.make_async_copy(ys_ref.at[pl.ds(0, tg)], buf.at[slot, k], sem.at[slot]).wait()
    acc = resid_ref[...]
    for k in range(TOP_K):
        acc = acc + gate_ref[:, k:k + 1] * buf[slot, k]
    y_o[...] = _layer_norm(acc, lng[...], lnb[...])


def _combine(resid, gate, dest, ys, lng, lnb):
    t = resid.shape[0]
    tg = min(COMBINE_TILE, t)
    nt = t // tg
    row = lambda w: pl.BlockSpec((tg, w), lambda i: (i, 0))
    return pl.pallas_call(
        functools.partial(_final_kernel, n=nt),
        grid=(nt,),
        in_specs=[row(D_MODEL), row(LANES),
                  pl.BlockSpec((tg * TOP_K,), lambda i: (i,), memory_space=pltpu.SMEM),
                  pl.BlockSpec((tg * TOP_K,), lambda i: (jnp.minimum(i + 1, nt - 1),), memory_space=pltpu.SMEM),
                  pl.BlockSpec(memory_space=pl.ANY),
                  _const_spec(lng.shape), _const_spec(lnb.shape)],
        out_specs=row(D_MODEL),
        out_shape=jax.ShapeDtypeStruct((t, D_MODEL), F32),
        scratch_shapes=[pltpu.VMEM((2, TOP_K, tg, ys.shape[1]), F32), pltpu.SemaphoreType.DMA((2,))],
        compiler_params=pltpu.CompilerParams(dimension_semantics=("arbitrary",), vmem_limit_bytes=VMEM_LIMIT),
        name="combine",
    )(resid, gate, dest, dest, ys, lng, lnb)


def _layer(x_prompt, x_sample, cache_k, cache_v, cache_logf, p_prompt, p_sample,
           w_in, b_in, lnv_g, lnv_b, w_s, b_s, w_pa, w_pb, w_o, ln1_g, ln1_b,
           w_router, b_router, w_gu, b_gu, w_dn, b_dn, w_ple_gate, b_ple_gate, w_ple_proj, ln2_g, ln2_b,
           dn_alpha):
    nb, seq, d = x_prompt.shape
    sb, new, _ = x_sample.shape
    past = cache_k.shape[1]
    tp, ts = nb * seq, sb * new
    vec = lambda a: a.reshape(1, -1).astype(F32)

    ws, bs = _proj_weights(w_in, b_in)
    lng, lnb = vec(lnv_g), vec(lnv_b)
    xp = x_prompt.reshape(tp, d)
    xsm = x_sample.reshape(ts, d)
    u_p, va_p, k_p, v_p, lf_p, g_p, qa_p, ka_p, vt_p = _project_prompt(xp, seq, ws, bs, lng, lnb)
    u_s, va_s, k_s, v_s, lf_s, g_s, q_s = _project_sample(xsm, ws, bs, lng, lnb)

    attn_p = _attend_prompt(qa_p, ka_p, vt_p, nb, seq)
    cache_lf = jnp.pad(cache_logf.astype(F32), ((0, 0), (0, 0), (0, LANES - N_HEADS)))
    attn_s = _attend_sample(q_s.reshape(sb, new, d), k_s.reshape(sb, new, d), v_s.reshape(sb, new, d),
                            lf_s.reshape(sb, new, LANES), cache_k.reshape(sb, past, d),
                            cache_v.reshape(sb, past, d), cache_lf)

    wr = jnp.pad(w_router.astype(F32), ((0, 0), (0, LANES - N_EXPERTS)))
    wr_hi = wr.astype(BF16)
    wr_lo = (wr - wr_hi.astype(F32)).astype(BF16)
    br = jnp.concatenate([b_router.astype(F32), jnp.full((LANES - N_EXPERTS,), NEG_BIG, F32)]).reshape(1, LANES)
    spatial = b_s.shape[1]
    bst = jnp.pad(b_s.astype(F32).T, ((0, 0), (0, LANES - b_s.shape[0])))
    post_consts = [w_s.astype(F32), bst, w_pa.astype(BF16), w_pb.astype(BF16), w_o.astype(BF16),
                   vec(ln1_g), vec(ln1_b), wr_hi, wr_lo, br,
                   w_ple_gate.astype(BF16), vec(b_ple_gate), w_ple_proj.astype(BF16)]
    base0 = jnp.zeros((1, LANES), F32)
    x1r_p, resid_p, topi_p, gate_p, rank_p, cnt_p = _post_attention(
        xp, u_p, va_p, g_p, attn_p, p_prompt.reshape(tp, -1), base0, post_consts, min(spatial, seq), dn_alpha)
    x1r_s, resid_s, topi_s, gate_s, rank_s, cnt = _post_attention(
        xsm, u_s, va_s.astype(BF16), g_s, attn_s, p_sample.reshape(ts, -1), cnt_p, post_consts,
        min(spatial, new), dn_alpha)

    counts = cnt[0, :N_EXPERTS].astype(I32)
    padded = (counts + EXPERT_BLOCK - 1) // EXPERT_BLOCK * EXPERT_BLOCK
    pad_end = jnp.cumsum(padded)
    pad_start = (pad_end - padded).astype(I32)
    n_assign = (tp + ts) * TOP_K
    nblk = -(-n_assign // EXPERT_BLOCK) + N_EXPERTS
    n_rows = nblk * EXPERT_BLOCK
    block_row0 = jnp.arange(nblk, dtype=I32) * EXPERT_BLOCK
    block_expert = jnp.minimum(jnp.sum(pad_end[None, :] <= block_row0[:, None], axis=1), N_EXPERTS - 1).astype(I32)
    n_used = (pad_end[-1:] // EXPERT_BLOCK).astype(I32)

    dest_of = lambda topi, rank: (jnp.take(pad_start, topi[:, :TOP_K]) + rank[:, :TOP_K]).reshape(-1)
    dest_p, dest_s = dest_of(topi_p, rank_p), dest_of(topi_s, rank_s)
    xs = _dispatch(pad_start + counts, padded - counts, n_used, x1r_p, x1r_s,
                   jnp.concatenate([dest_p, dest_s]), n_rows)

    ys = _experts(block_expert, n_used, xs, w_gu.astype(F32), b_gu.astype(F32)[:, None, :],
                  w_dn.astype(F32), b_dn.astype(F32)[:, None, :])

    l2g, l2b = vec(ln2_g), vec(ln2_b)
    y_p = _combine(resid_p, gate_p, dest_p, ys, l2g, l2b)
    y_s = _combine(resid_s, gate_s, dest_s, ys, l2g, l2b)

    shp = lambda a, b_, n: a.reshape(b_, n, N_HEADS, HEAD_DIM)
    return (y_p.reshape(nb, seq, d), y_s.reshape(sb, new, d),
            shp(k_p, nb, seq), shp(v_p, nb, seq), lf_p.reshape(nb, seq, N_HEADS),
            shp(k_s, sb, new), shp(v_s, sb, new), lf_s[:, :N_HEADS].reshape(sb, new, N_HEADS),
            va_s.reshape(sb, new, N_GROUPS, d // N_GROUPS))


def kernel(x_prompt, x_sample, cache_fox_k, cache_fox_v, cache_fox_logf, p_prompt, p_sample, w_in, b_in, lnv_g, lnv_b, w_s, b_s, w_pa, w_pb, w_o, ln1_g, ln1_b, w_router, b_router, w_gu, b_gu, w_dn, b_dn, w_ple_gate, b_ple_gate, w_ple_proj, ln2_g, ln2_b):
    depth = w_in.shape[0]
    assert depth == 1, "the layer loop below carries one layer"
    dn_alpha = float((2 * depth) ** 0.25)
    outs = _layer(x_prompt, x_sample, cache_fox_k[0], cache_fox_v[0], cache_fox_logf[0], p_prompt[0], p_sample[0],
                  w_in[0], b_in[0], lnv_g[0], lnv_b[0], w_s[0], b_s[0], w_pa[0], w_pb[0], w_o[0], ln1_g[0], ln1_b[0],
                  w_router[0], b_router[0], w_gu[0], b_gu[0], w_dn[0], b_dn[0], w_ple_gate[0], b_ple_gate[0],
                  w_ple_proj[0], ln2_g[0], ln2_b[0], dn_alpha)
    y_p, y_s, k_p, v_p, lf_p, k_s, v_s, lf_s, va_s = outs
    lead = lambda a: a[None]
    return (y_p, y_s, lead(k_p), lead(v_p), lead(lf_p), lead(k_s), lead(v_s), lead(lf_s), lead(va_s))
```

```python
import functools

import jax
import jax.numpy as jnp
import numpy as np
from jax import lax
from jax.experimental import pallas as pl
from jax.experimental.pallas import tpu as pltpu

F32 = jnp.float32
BF16 = jnp.bfloat16
I32 = jnp.int32

LANES = 128
SUBLANES = 8
D_MODEL = 1024
N_HEADS = 16
HEAD_DIM = 64
N_GROUPS = 8
N_EXPERTS = 32
TOP_K = 4
SWIGLU_ALPHA = 1.702
SWIGLU_LIMIT = 7.0
LN_EPS = 1e-5
NEG_BIG = -1e30
LOG2_E = 1.4426950408889634
EXP_ROWS = 32
DEN_ROWS = 16

TOKEN_TILE = 256
Q_TILE = 512
SAMPLE_CHUNK = 512
EXPERT_BLOCK = 512
DISPATCH_TILE = 1024
COMBINE_TILE = 256
DMA_UNROLL = 8
VMEM_LIMIT = 56 * 1024 * 1024

F_MID_SHIFT = N_HEADS
F_LO_SHIFT = 2 * N_HEADS
F_ONE_LANE = 3 * N_HEADS


def _dot(a, b):
    return jnp.dot(a, b, preferred_element_type=F32)


def _dot_nt(a, b):
    return lax.dot_general(a, b, (((1,), (1,)), ((), ())), preferred_element_type=F32)


def _layer_norm(x, g, b):
    mu = jnp.mean(x, axis=-1, keepdims=True)
    xc = x - mu
    var = jnp.mean(xc * xc, axis=-1, keepdims=True)
    return xc * lax.rsqrt(var + LN_EPS) * g + b


def _log_sigmoid(z):
    return jnp.minimum(z, 0.0) - jnp.log1p(jnp.exp(-jnp.abs(z)))


def _split3(x):
    hi = x.astype(BF16).astype(F32)
    r = x - hi
    mid = r.astype(BF16).astype(F32)
    lo = (r - mid).astype(BF16).astype(F32)
    return hi, mid, lo


def _tri(n, strict):
    r = lax.broadcasted_iota(I32, (n, n), 0)
    c = lax.broadcasted_iota(I32, (n, n), 1)
    keep = (r > c) if strict else (r >= c)
    return jnp.where(keep, 1.0, 0.0).astype(F32)


def _cumsum_rows(x):
    n = x.shape[0]
    hi, mid, lo = _split3(x)
    cat = jnp.concatenate([hi, mid, lo], axis=1).astype(BF16)
    c = _dot(_tri(n, strict=False).astype(BF16), cat)
    return (c[:, 2 * LANES:] + c[:, LANES:2 * LANES]) + c[:, :LANES]


def _proj_sections(x_ref, wuv, wkv, wf, wg, buv, bkv, bf_, bg, lng, lnb, u_o, k_o, v_o, g_o):
    xb = x_ref[...].astype(BF16)
    pu = _dot(xb, wuv[:, :D_MODEL]) + buv[:, :D_MODEL]
    u_o[...] = jax.nn.gelu(pu).astype(BF16)
    pv = _dot(xb, wuv[:, D_MODEL:]) + buv[:, D_MODEL:]
    va = _layer_norm(jax.nn.gelu(pv), lng[...], lnb[...])
    k = _dot(xb, wkv[:, :D_MODEL]) + bkv[:, :D_MODEL]
    k_o[...] = k.reshape(k_o.shape)
    v =_dot(xb, wkv[:, D_MODEL:]) + bkv[:, D_MODEL:]
    v_o[...] = v.reshape(v_o.shape)
    g_o[...] = jax.nn.sigmoid(_dot(xb, wg[...]) + bg[...]).astype(BF16)
    f = _dot(xb, wf[...]) + bf_[...]
    lane = lax.broadcasted_iota(I32, f.shape, 1)
    lf = jnp.where(lane < N_HEADS, _log_sigmoid(f), 0.0)
    return xb, va, k, v, lf


def _proj_prompt_kernel(x_ref, wuv, wq, wkv, wf, wg, buv, bq, bkv, bf_, bg, lng, lnb, eq, ek,
                        u_o, va_o, k_o, v_o, lf_o, g_o, qa_o, ka_o, vt_o, carry, *, tiles_per_seq):
    i = pl.program_id(0)
    xb, va, k, v, lf = _proj_sections(x_ref, wuv, wkv, wf, wg, buv, bkv, bf_, bg, lng, lnb, u_o, k_o, v_o, g_o)
    va_o[...] = va.astype(BF16)
    vt_o[0, 0] = v.T.astype(BF16)
    lf_o[...] = lf[:, :N_HEADS]

    @pl.when(i % tiles_per_seq == 0)
    def _():
        carry[...] = jnp.zeros_like(carry)

    tm = lf.shape[0]
    fcum = carry[...] + _cumsum_rows(lf)
    carry[...] = fcum[tm - 1:tm, :]

    hi, mid, lo = _split3(fcum * LOG2_E)
    lane = lax.broadcasted_iota(I32, fcum.shape, 1)
    pack = (hi + pltpu.roll(mid, F_MID_SHIFT, 1) + pltpu.roll(lo, F_LO_SHIFT, 1)
            + jnp.where(lane == F_ONE_LANE, 1.0, 0.0)).astype(BF16)
    add_q = _dot(pack, eq[...])
    add_k = _dot(pack, ek[...])
    q = (_dot(xb, wq[...]) + bq[...]) * (HEAD_DIM ** -0.5 * LOG2_E)
    low =lane < HEAD_DIM
    for g in range(N_GROUPS):
        sl = slice(g * LANES, (g + 1) * LANES)
        ev = slice(2 * g * LANES, (2 * g + 1) * LANES)
        od = slice((2 * g + 1) * LANES, (2 * g + 2) * LANES)
        qa_o[:, ev] = jnp.where(low, q[:, sl], add_q[:, ev]).astype(BF16)
        qa_o[:, od] = jnp.where(low, add_q[:, od], q[:, sl]).astype(BF16)
        ka_o[:, ev] = jnp.where(low, k[:, sl], add_k[:, ev]).astype(BF16)
        ka_o[:, od] = jnp.where(low, add_k[:, od], k[:, sl]).astype(BF16)


def _proj_sample_kernel(x_ref, wuv, wq, wkv, wf, wg, buv, bq, bkv, bf_, bg, lng, lnb,
                        u_o, va_o, k_o, v_o, lf_o, g_o, q_o):
    xb, va, _, _, lf = _proj_sections(x_ref, wuv, wkv, wf, wg, buv, bkv, bf_, bg, lng, lnb, u_o, k_o, v_o, g_o)
    va_o[...] = va
    lf_o[...] = lf
    q_o[...] = (_dot(xb, wq[...]) + bq[...]) * (HEAD_DIM ** -0.5)


def _const_spec(shape):
    nd = len(shape)
    return pl.BlockSpec(shape, lambda *_: (0,) * nd, pipeline_mode=pl.Buffered(1))


def _proj_weights(w_in, b_in):
    d = D_MODEL
    off_q, off_k, off_f = 2 * d, 3 * d, 5 * d
    off_ga = off_f + N_HEADS
    wuv = w_in[:, :off_q].astype(BF16)
    wq = w_in[:, off_q:off_k].astype(BF16)
    wkv = w_in[:, off_k:off_f].astype(BF16)
    wf = jnp.pad(w_in[:, off_f:off_ga], ((0, 0), (0, LANES - N_HEADS))).astype(BF16)
    wg = w_in[:, off_ga:].astype(BF16)
    b = b_in.reshape(1, -1).astype(F32)
    buv, bq, bkv = b[:, :off_q], b[:, off_q:off_k], b[:, off_k:off_f]
    bf_ = jnp.pad(b[:, off_f:off_ga], ((0, 0), (0, LANES - N_HEADS)))
    bg = b[:, off_ga:]
    return (wuv, wq, wkv, wf, wg), (buv, bq, bkv, bf_, bg)


def _spread_matrices():
    eq_np = np.zeros((LANES, 2 * D_MODEL), np.float32)
    ek_np = np.zeros((LANES, 2 * D_MODEL), np.float32)
    for h in range(N_HEADS):
        base = LANES * h + (HEAD_DIM if h % 2 == 0 else 0)
        for c, shift in enumerate((0, F_MID_SHIFT, F_LO_SHIFT)):
            eq_np[shift + h, base + c] = 1.0
            eq_np[F_ONE_LANE, base + 3 + c] = 1.0
            ek_np[F_ONE_LANE, base + c] = 1.0
            ek_np[shift + h, base + 3 + c] = -1.0
    return jnp.asarray(eq_np, BF16), jnp.asarray(ek_np, BF16)


def _project_prompt(x2d, seq, ws, bs, lng, lnb):
    t = x2d.shape[0]
    tm = min(TOKEN_TILE, seq)
    nt = t // tm
    nb = t // seq
    eq, ek = _spread_matrices()
    consts = list(ws) + list(bs) + [lng, lnb, eq, ek]
    row = lambda w: pl.BlockSpec((tm, w), lambda i: (i, 0))
    out_shape = (
        jax.ShapeDtypeStruct((t, D_MODEL), BF16),
        jax.ShapeDtypeStruct((t, D_MODEL), BF16),
        jax.ShapeDtypeStruct((t, N_HEADS, HEAD_DIM), F32),
        jax.ShapeDtypeStruct((t, N_HEADS, HEAD_DIM), F32),
        jax.ShapeDtypeStruct((t, N_HEADS), F32),
        jax.ShapeDtypeStruct((t, 2 * D_MODEL), BF16),
        jax.ShapeDtypeStruct((t, 2 * D_MODEL), BF16),
        jax.ShapeDtypeStruct((t, 2 * D_MODEL), BF16),
        jax.ShapeDtypeStruct((nb, seq // tm, D_MODEL, tm), BF16),
    )
    tiles_per_seq = seq // tm
    heads = pl.BlockSpec((tm, N_HEADS, HEAD_DIM), lambda i: (i, 0, 0))
    out_specs = (row(D_MODEL), row(D_MODEL), heads, heads, row(N_HEADS),
                 row(2 * D_MODEL), row(2 * D_MODEL), row(2 * D_MODEL),
                 pl.BlockSpec((1, 1, D_MODEL, tm), lambda i: (i // tiles_per_seq, i % tiles_per_seq, 0, 0)))
    return pl.pallas_call(
        functools.partial(_proj_prompt_kernel, tiles_per_seq=tiles_per_seq),
        grid=(nt,),
        in_specs=[row(D_MODEL)] + [_const_spec(c.shape) for c in consts],
        out_specs=out_specs,
        out_shape=out_shape,
        scratch_shapes=[pltpu.VMEM((1, LANES), F32)],
        compiler_params=pltpu.CompilerParams(dimension_semantics=("arbitrary",), vmem_limit_bytes=VMEM_LIMIT),
        name="proj_prompt",
    )(x2d, *consts)


def _project_sample(x2d, ws, bs, lng, lnb):
    t = x2d.shape[0]
    tm = min(TOKEN_TILE, t)
    nt = t // tm
    consts = list(ws) + list(bs) + [lng, lnb]
    row = lambda w: pl.BlockSpec((tm, w), lambda i: (i, 0))
    out_shape = (
        jax.ShapeDtypeStruct((t, D_MODEL), BF16),
        jax.ShapeDtypeStruct((t, D_MODEL), F32),
        jax.ShapeDtypeStruct((t, D_MODEL), F32),
        jax.ShapeDtypeStruct((t, D_MODEL), F32),
        jax.ShapeDtypeStruct((t, LANES), F32),
        jax.ShapeDtypeStruct((t, 2 * D_MODEL), BF16),
        jax.ShapeDtypeStruct((t, D_MODEL), F32),
    )
    out_specs = (row(D_MODEL), row(D_MODEL), row(D_MODEL), row(D_MODEL), row(LANES), row(2 * D_MODEL), row(D_MODEL))
    return pl.pallas_call(
        _proj_sample_kernel,
        grid=(nt,),
        in_specs=[row(D_MODEL)] + [_const_spec(c.shape) for c in consts],
        out_specs=out_specs,
        out_shape=out_shape,
        compiler_params=pltpu.CompilerParams(dimension_semantics=("arbitrary",), vmem_limit_bytes=VMEM_LIMIT),
        name="proj_sample",
    )(x2d, *consts)


def _attn_prompt_kernel(q_ref, k_ref, vt_ref, o_ref, m_ref, acc_ref, ot_ref,
                        s0, s1, p0, p1, a0, a1, c0, c1, *, tq, tk):
    qi = pl.program_id(2)
    assert tq == 2 * tk, "the pipeline below is written for two diagonal spans per query tile"
    m_ref[...] = jnp.full(m_ref.shape, NEG_BIG, F32)
    acc_ref[...] = jnp.zeros(acc_ref.shape, F32)
    p1[...] = jnp.zeros(p1.shape, BF16)
    a1[...] = jnp.ones(a1.shape, F32)
    tv = vt_ref.shape[3]
    ones = jnp.ones((DEN_ROWS, tv), BF16)
    lanes = lambda hh: slice(hh * LANES, (hh + 1) * LANES)

    def score(t, s_slot, c_slot):
        start = pl.multiple_of(t * tk, tk)
        for hh in range(2):
            st = _dot_nt(k_ref[pl.ds(start, tk), lanes(hh)], q_ref[:, lanes(hh)])
            s_slot[hh] = st
            c_slot[hh] = jnp.max(st, axis=0, keepdims=True)

    def softmax(s_slot, c_slot, p_slot, a_slot, mask_offset):
        for hh in range(2):
            if mask_offset is None:
                col_max = c_slot[hh]
            else:
                key_row = lax.broadcasted_iota(I32, (tk, tq), 0)
                qry_col = lax.broadcasted_iota(I32, (tk, tq), 1)
                s_slot[hh] = jnp.where(key_row + mask_offset <= qry_col, s_slot[hh], NEG_BIG)
                col_max = jnp.max(s_slot[hh], axis=0, keepdims=True)
            m_old = m_ref[hh]
            m_new = jnp.maximum(m_old, col_max)
            a_slot[hh] = jnp.exp2(m_old - m_new)
            for r in range(0, tk, EXP_ROWS):
                rows = slice(r, r + EXP_ROWS)
                p_slot[hh, rows, :] = jnp.exp2((s_slot[hh, rows, :] - m_new).astype(BF16))
            m_ref[hh] = m_new

    def value(t, p_slot, a_slot):
        span = jnp.maximum(t, 0)
        for hh in range(2):
            pv = None
            for d in range(tk // tv):
                vt = vt_ref[0, span * (tk // tv) + d, hh * HEAD_DIM:(hh + 1) * HEAD_DIM, :]
                part = _dot(jnp.concatenate([vt, ones], axis=0), p_slot[hh, d * tv:(d + 1) * tv, :])
                pv = part if pv is None else pv + part
            acc_ref[hh] = a_slot[hh] * acc_ref[hh] + pv

    score(0, s0, c0)

    def two_steps(ii):
        t = 2 * ii + 1
        value(t - 2, p1, a1)
        score(t, s1, c1)
        softmax(s0, c0, p0, a0, None)
        value(t - 1, p0, a0)
        score(t + 1, s0, c0)
        softmax(s1, c1, p1, a1, None)

    def body(jj, c):
        two_steps(2 * jj)
        two_steps(2 * jj + 1)
        return c

    lax.fori_loop(0, qi // 2, body, 0)

    @pl.when(qi % 2 == 1)
    def _():
        two_steps(qi - 1)

    t = 2 * qi + 1
    value(t - 2, p1, a1)
    score(t, s1, c1)
    softmax(s0, c0, p0, a0, 0)
    value(t - 1, p0, a0)
    softmax(s1, c1, p1, a1, tk)
    value(t, p1, a1)
    for hh in range(2):
        ot_ref[hh * HEAD_DIM:(hh + 1) * HEAD_DIM, :] = acc_ref[hh, :HEAD_DIM] / acc_ref[hh, HEAD_DIM:HEAD_DIM + 1]
    o_ref[...] = ot_ref[...].T.astype(BF16)


def _attend_prompt(q_aug, k_aug, v_t, nb, seq):
    t = q_aug.shape[0]
    tv = v_t.shape[3]
    tq = min(Q_TILE, seq)
    tk = tq // 2
    nq = seq // tq
    return pl.pallas_call(
        functools.partial(_attn_prompt_kernel, tq=tq, tk=tk),
        grid=(nb, N_GROUPS, nq),
        in_specs=[
            pl.BlockSpec((tq, 2 * LANES), lambda b, g, i: (b * nq + i, g)),
            pl.BlockSpec((seq, 2 * LANES), lambda b, g, i: (b, g)),
            pl.BlockSpec((1, seq // tv, LANES, tv), lambda b, g, i: (b, 0, g, 0)),
        ],
        out_specs=pl.BlockSpec((tq, LANES), lambda b, g, i: (b * nq + i, g)),
        out_shape=jax.ShapeDtypeStruct((t, D_MODEL), BF16),
        scratch_shapes=[pltpu.VMEM((2, 1, tq), F32), pltpu.VMEM((2, HEAD_DIM + DEN_ROWS, tq), F32),
                        pltpu.VMEM((LANES, tq), F32),
                        pltpu.VMEM((2, tk, tq), F32), pltpu.VMEM((2, tk, tq), F32),
                        pltpu.VMEM((2, tk, tq), BF16), pltpu.VMEM((2, tk, tq), BF16),
                        pltpu.VMEM((2, 1, tq), F32), pltpu.VMEM((2, 1, tq), F32),
                        pltpu.VMEM((2, 1, tq), F32), pltpu.VMEM((2, 1, tq), F32)],
        compiler_params=pltpu.CompilerParams(
            dimension_semantics=("arbitrary", "arbitrary", "arbitrary"), vmem_limit_bytes=VMEM_LIMIT),
        name="attn_prompt",
    )(q_aug, k_aug, v_t)


def _attn_sample_kernel(q_ref, k_ref, v_ref, lf_ref, ck_ref, cv_ref, clf_ref, o_ref,
                        fct_sc, fn_sc, m_sc, l_sc, acc_sc, *, n_chunks):
    c = pl.program_id(1)
    chunk = ck_ref.shape[1]
    new = q_ref.shape[1]

    @pl.when(c == 0)
    def _():
        carry = jnp.zeros((1, LANES), F32)
        for cc in range(n_chunks):
            fc = carry + _cumsum_rows(clf_ref[0, cc * chunk:(cc + 1) * chunk, :])
            carry = fc[chunk - 1:chunk, :]
            fct_sc[cc] = fc.T
        fn_sc[...] = carry + _cumsum_rows(lf_ref[0])
        m_sc[...] = jnp.full(m_sc.shape, NEG_BIG, F32)
        l_sc[...] = jnp.zeros(l_sc.shape, F32)
        acc_sc[...] = jnp.zeros(acc_sc.shape, F32)

    lane = lax.broadcasted_iota(I32, (new, LANES), 1)
    f_new = fn_sc[...]

    def attend(keys, values, f_keys_t, mask):
        for g in range(N_GROUPS):
            sl = slice(g * LANES, (g + 1) * LANES)
            qg = q_ref[0, :, sl]
            for hh in range(2):
                h = 2 * g + hh
                mine = (lane < HEAD_DIM) if hh == 0 else (lane >= HEAD_DIM)
                qm = jnp.where(mine, qg, 0.0).astype(BF16)
                s = _dot_nt(qm, keys[:, sl]) + f_new[:, h:h + 1] - f_keys_t[h:h + 1, :]
                if mask is not None:
                    s = jnp.where(mask, s, NEG_BIG)
                m_old = m_sc[h]
                m_new = jnp.maximum(m_old, jnp.max(s, axis=1, keepdims=True))
                alpha = jnp.exp(m_old - m_new)
                p = jnp.exp(s - m_new)
                l_sc[h] = alpha * l_sc[h] + jnp.sum(p, axis=1, keepdims=True)
                acc_sc[h] = alpha * acc_sc[h] + _dot(p.astype(BF16), values[:, sl])
                m_sc[h] = m_new

    attend(ck_ref[0].reshape(chunk, D_MODEL).astype(BF16), cv_ref[0].reshape(chunk, D_MODEL).astype(BF16),
           fct_sc[c], None)

    @pl.when(c == n_chunks - 1)
    def _():
        causal = lax.broadcasted_iota(I32, (new, new), 1) <= lax.broadcasted_iota(I32, (new, new), 0)
        attend(k_ref[0].astype(BF16), v_ref[0].astype(BF16), f_new.T, causal)
        for g in range(N_GROUPS):
            even = acc_sc[2 * g] / l_sc[2 * g]
            odd = acc_sc[2 * g + 1] / l_sc[2 * g + 1]
            o_ref[:, g * LANES:(g + 1) * LANES] = jnp.where(lane < HEAD_DIM, even, odd).astype(BF16)


def _attend_sample(q, k, v, lf, cache_k, cache_v, cache_lf):
    nb, new, _ = q.shape
    past = cache_k.shape[1]
    chunk = min(SAMPLE_CHUNK, past)
    n_chunks = past // chunk
    blk = lambda n, w: pl.BlockSpec((1, n, w), lambda b, c: (b, 0, 0))
    cache = pl.BlockSpec((1, chunk, N_HEADS, HEAD_DIM), lambda b, c: (b, c, 0, 0))
    return pl.pallas_call(
        functools.partial(_attn_sample_kernel, n_chunks=n_chunks),
        grid=(nb, n_chunks),
        in_specs=[blk(new, D_MODEL), blk(new, D_MODEL), blk(new, D_MODEL), blk(new, LANES),
                  cache, cache, blk(past, LANES)],
        out_specs=pl.BlockSpec((new, D_MODEL), lambda b, c: (b, 0)),
        out_shape=jax.ShapeDtypeStruct((nb * new, D_MODEL), BF16),
        scratch_shapes=[pltpu.VMEM((n_chunks, LANES, chunk), F32), pltpu.VMEM((new, LANES), F32),
                        pltpu.VMEM((N_HEADS, new, 1), F32), pltpu.VMEM((N_HEADS, new, 1), F32),
                        pltpu.VMEM((N_HEADS, new, LANES), F32)],
        compiler_params=pltpu.CompilerParams(dimension_semantics=("arbitrary", "arbitrary"),
                                             vmem_limit_bytes=VMEM_LIMIT),
        name="attn_sample",
    )(q, k, v, lf, cache_k, cache_v, cache_lf)


def _post_kernel(x_ref, u_ref, va_ref, g_ref, at_ref, p_ref, base_ref, ws_ref, bst_ref, wpa, wpb, wo,
                 l1g, l1b, wr_hi, wr_lo, br, wpg, bpg, wpp,
                 x1_o, resid_o, topi_o, gate_o, rank_o, cnt_o, a_sc, base_sc, *, chunk, dn_alpha):
    i = pl.program_id(0)
    tm = x_ref.shape[0]

    @pl.when(i == 0)
    def _():
        base_sc[...] = base_ref[...]

    tri = _tri(chunk, strict=False)
    for g in range(N_GROUPS):
        sl = slice(g * LANES, (g + 1) * LANES)
        wm = (ws_ref[g, :chunk, :chunk] * tri).astype(BF16)
        bias = bst_ref[:chunk, g:g + 1]
        for c in range(tm // chunk):
            rows = slice(c * chunk, (c + 1) * chunk)
            sg = _dot(wm, va_ref[rows, sl]) + bias
            a_sc[rows, sl] = (u_ref[rows, sl].astype(F32) * sg).astype(BF16)

    pa = _dot(a_sc[...], wpa[...])
    pb = _dot(at_ref[...], wpb[...])
    merged = g_ref[:, :D_MODEL].astype(F32) * pa + g_ref[:, D_MODEL:].astype(F32) * pb
    x1 = _layer_norm(dn_alpha * x_ref[...] + _dot(merged.astype(BF16), wo[...]), l1g[...], l1b[...])
    x1b = x1.astype(BF16)
    x1_o[...] = x1

    ple = jax.nn.sigmoid(_dot(x1b, wpg[...]) + bpg[...]) * _dot(p_ref[...].astype(BF16), wpp[...])
    resid_o[...] = dn_alpha * x1 + ple

    x1_lo = (x1 - x1b.astype(F32)).astype(BF16)
    logits = (_dot(x1b, wr_hi[...]) + (_dot(x1b, wr_lo[...]) + _dot(x1_lo, wr_hi[...]))) + br[...]
    lane = lax.broadcasted_iota(I32, (tm, LANES), 1)
    work = logits
    vals, idxs = [], []
    for _ in range(TOP_K):
        mk = jnp.max(work, axis=1, keepdims=True)
        ik = jnp.min(jnp.where(work == mk, lane, LANES), axis=1, keepdims=True)
        vals.append(mk)
        idxs.append(ik)
        work = jnp.where(lane == ik, -jnp.inf, work)
    exps = [jnp.exp(v - vals[0]) for v in vals]
    den = exps[0] + exps[1] + exps[2] + exps[3]

    lower = _tri(tm, strict=True).astype(BF16)
    run = base_sc[...]
    topi = jnp.zeros((tm, LANES), I32)
    gate = jnp.zeros((tm, LANES), F32)
    rank = jnp.zeros((tm, LANES), F32)
    for k in range(TOP_K):
        onehot = jnp.where(lane == idxs[k], 1.0, 0.0)
        before = _dot(lower, onehot.astype(BF16))
        rk = jnp.sum(onehot * (before + run), axis=1, keepdims=True)
        topi = jnp.where(lane == k, idxs[k], topi)
        gate = jnp.where(lane == k, exps[k] / den, gate)
        rank = jnp.where(lane == k, rk, rank)
        run = run + jnp.sum(onehot, axis=0, keepdims=True)
    base_sc[...] = run
    topi_o[...] = topi
    gate_o[...] = gate
    rank_o[...] = rank.astype(I32)
    cnt_o[...] = run


def _post_attention(x2d, u, va, gates, attn, p2d, base, consts, chunk, dn_alpha):
    t = x2d.shape[0]
    tm = min(TOKEN_TILE, t)
    nt = t // tm
    row = lambda w: pl.BlockSpec((tm, w), lambda i: (i, 0))
    ple_dim = p2d.shape[1]
    out_shape = (
        jax.ShapeDtypeStruct((t, D_MODEL), F32),
        jax.ShapeDtypeStruct((t, D_MODEL), F32),
        jax.ShapeDtypeStruct((t, LANES), I32),
        jax.ShapeDtypeStruct((t, LANES), F32),
        jax.ShapeDtypeStruct((t, LANES), I32),
        jax.ShapeDtypeStruct((1, LANES), F32),
    )
    out_specs = (row(D_MODEL), row(D_MODEL), row(LANES), row(LANES), row(LANES),
                 pl.BlockSpec((1, LANES), lambda i: (0, 0)))
    return pl.pallas_call(
        functools.partial(_post_kernel, chunk=chunk, dn_alpha=dn_alpha),
        grid=(nt,),
        in_specs=[row(D_MODEL), row(D_MODEL), row(D_MODEL), row(2 * D_MODEL), row(D_MODEL), row(ple_dim),
                  _const_spec(base.shape)] + [_const_spec(c.shape) for c in consts],
        out_specs=out_specs,
        out_shape=out_shape,
        scratch_shapes=[pltpu.VMEM((tm, D_MODEL), BF16), pltpu.VMEM((1, LANES), F32)],
        compiler_params=pltpu.CompilerParams(dimension_semantics=("arbitrary",), vmem_limit_bytes=VMEM_LIMIT),
        name="post_attention",
    )(x2d, u, va, gates, attn, p2d, base, *consts)


def _dispatch_kernel(fill_start_ref, fill_len_ref, nu_ref, xp_ref, xn_ref, dest_ref, xs_out, zero_sc, sem, zsem,
                     *, prompt_steps, n_blocks):
    i = pl.program_id(0)
    td = xp_ref.shape[0]

    @pl.when(i == 0)
    def _():
        zero_sc[...] = jnp.zeros(zero_sc.shape, zero_sc.dtype)

        def zero_fill(wait):
            def go(copy):
                copy.wait() if wait else copy.start()

            def per_expert(e, c):
                n = fill_len_ref[e]
                base = fill_start_ref[e]
                head = jnp.bitwise_and(n, SUBLANES - 1)
                for r in range(SUBLANES - 1):
                    @pl.when(r < head)
                    def _(r=r):
                        go(pltpu.make_async_copy(zero_sc.at[pl.ds(0, 1)], xs_out.at[pl.ds(base + r, 1)], zsem))

                piece = EXPERT_BLOCK // 2
                while piece >= SUBLANES:
                    offset = pl.multiple_of(base + head + jnp.bitwise_and(n, -(2 * piece)), SUBLANES)

                    @pl.when(jnp.bitwise_and(n, piece) != 0)
                    def _(piece=piece, offset=offset):
                        go(pltpu.make_async_copy(zero_sc.at[pl.ds(0, piece)], xs_out.at[pl.ds(offset, piece)], zsem))

                    piece //= 2
                return c

            lax.fori_loop(0, N_EXPERTS, per_expert, 0)

            def per_block(j, c):
                row0 = pl.multiple_of(j * EXPERT_BLOCK, EXPERT_BLOCK)
                go(pltpu.make_async_copy(zero_sc, xs_out.at[pl.ds(row0, EXPERT_BLOCK)], zsem))
                return c

            lax.fori_loop(nu_ref[0], n_blocks, per_block, 0)

        zero_fill(wait=False)
        zero_fill(wait=True)

    def scatter(x_ref):
        def body(j, c):
            for u in range(DMA_UNROLL):
                tok = j * (DMA_UNROLL // TOP_K) + u // TOP_K
                pltpu.make_async_copy(x_ref.at[pl.ds(tok, 1)], xs_out.at[pl.ds(dest_ref[j * DMA_UNROLL + u], 1)],
                                      sem).start(priority=u % 2)
            return c

        lax.fori_loop(0, td * TOP_K // DMA_UNROLL, body, 0)
        for _ in range(TOP_K):
            pltpu.make_async_copy(x_ref, xs_out.at[pl.ds(0, td)], sem).wait()

    @pl.when(i < prompt_steps)
    def _():
        scatter(xp_ref)

    @pl.when(i >= prompt_steps)
    def _():
        scatter(xn_ref)


def _dispatch(fill_start, fill_len, n_used, x1_prompt, x1_sample, dest, n_rows):
    tp, w = x1_prompt.shape
    ts = x1_sample.shape[0]
    td = min(DISPATCH_TILE, ts)
    prompt_steps, sample_steps = tp // td, ts // td
    grid_spec = pltpu.PrefetchScalarGridSpec(
        num_scalar_prefetch=3,
        grid=(prompt_steps + sample_steps,),
        in_specs=[pl.BlockSpec((td, w), lambda i, *_: (jnp.minimum(i, prompt_steps - 1), 0)),
                  pl.BlockSpec((td, w), lambda i, *_: (jnp.maximum(i - prompt_steps, 0), 0)),
                  pl.BlockSpec((td * TOP_K,), lambda i, *_: (i,), memory_space=pltpu.SMEM)],
        out_specs=pl.BlockSpec(memory_space=pl.ANY),
        scratch_shapes=[pltpu.VMEM((EXPERT_BLOCK, w), x1_prompt.dtype),
                        pltpu.SemaphoreType.DMA(()), pltpu.SemaphoreType.DMA(())],
    )
    return pl.pallas_call(
        functools.partial(_dispatch_kernel, prompt_steps=prompt_steps, n_blocks=n_rows // EXPERT_BLOCK),
        grid_spec=grid_spec,
        out_shape=jax.ShapeDtypeStruct((n_rows, w), x1_prompt.dtype),
        compiler_params=pltpu.CompilerParams(dimension_semantics=("arbitrary",), vmem_limit_bytes=VMEM_LIMIT),
        name="dispatch",
    )(fill_start, fill_len, n_used, x1_prompt, x1_sample, dest)


def _expert_kernel(be_ref, nu_ref, xs_ref, wgu, bgu, wdn, bdn, ys_o, wgu_bf, wdn_bf):
    j = pl.program_id(0)
    used = j < nu_ref[0]
    new_expert = jnp.logical_or(j == 0, be_ref[j] != be_ref[jnp.maximum(j - 1, 0)])

    @pl.when(jnp.logical_not(used))
    def _():
        ys_o[...] = jnp.zeros(ys_o.shape, ys_o.dtype)

    @pl.when(jnp.logical_and(used, new_expert))
    def _():
        wgu_bf[...] = wgu[0].astype(BF16)
        wdn_bf[...] = wdn[0].astype(BF16)

    @pl.when(used)
    def _():
        h = _dot(xs_ref[...].astype(BF16), wgu_bf[...]) + bgu[0]
        d_e = h.shape[1] // 2
        gate = jnp.minimum(h[:, :d_e], SWIGLU_LIMIT)
        up = jnp.clip(h[:, d_e:], -SWIGLU_LIMIT, SWIGLU_LIMIT)
        glu = gate * jax.nn.sigmoid(gate * SWIGLU_ALPHA)
        ys_o[...] = _dot(((up + 1.0) * glu).astype(BF16), wdn_bf[...]) + bdn[0]


def _experts(block_expert, n_used, xs, wgu, bgu, wdn, bdn):
    rows, w = xs.shape
    nblk = rows // EXPERT_BLOCK
    d_in, d_gu = wgu.shape[1], wgu.shape[2]
    d_e, d_out = wdn.shape[1], wdn.shape[2]
    grid_spec = pltpu.PrefetchScalarGridSpec(
        num_scalar_prefetch=2,
        grid=(nblk,),
        in_specs=[
            pl.BlockSpec((EXPERT_BLOCK, w), lambda j, be, nu: (j, 0)),
            pl.BlockSpec((1, d_in, d_gu), lambda j, be, nu: (be[j], 0, 0)),
            pl.BlockSpec((1, 1, d_gu), lambda j, be, nu: (be[j], 0, 0)),
            pl.BlockSpec((1, d_e, d_out), lambda j, be, nu: (be[j], 0, 0)),
            pl.BlockSpec((1, 1, d_out), lambda j, be, nu: (be[j], 0, 0)),
        ],
        out_specs=pl.BlockSpec((EXPERT_BLOCK, d_out), lambda j, be, nu: (j, 0)),
        scratch_shapes=[pltpu.VMEM((d_in, d_gu), BF16), pltpu.VMEM((d_e, d_out), BF16)],
    )
    return pl.pallas_call(
        _expert_kernel,
        grid_spec=grid_spec,
        out_shape=jax.ShapeDtypeStruct((rows, d_out), F32),
        compiler_params=pltpu.CompilerParams(dimension_semantics=("arbitrary",), vmem_limit_bytes=VMEM_LIMIT),
        name="experts",
    )(block_expert, n_used, xs, wgu, bgu, wdn, bdn)


def _final_kernel(resid_ref, gate_ref, dest_ref, dest_next_ref, ys_ref, lng, lnb, y_o, buf, sem, *, n):
    i = pl.program_id(0)
    tg = resid_ref.shape[0]

    def gather(idx_ref, slot):
        def body(j, c):
            for u in range(DMA_UNROLL):
                tok = j * (DMA_UNROLL // TOP_K) + u // TOP_K
                pltpu.make_async_copy(ys_ref.at[pl.ds(idx_ref[j * DMA_UNROLL + u], 1)],
                                      buf.at[slot, u % TOP_K, pl.ds(tok, 1)], sem.at[slot]).start(priority=u % 2)
            return c

        lax.fori_loop(0, tg * TOP_K // DMA_UNROLL, body, 0)

    @pl.when(i == 0)
    def _():
        gather(dest_ref, 0)

    @pl.when(i + 1 < n)
    def _():
        gather(dest_next_ref, (i + 1) % 2)

    slot = i % 2
    for k in range(TOP_K):
        pltpu.make_async_copy(ys_ref.at[pl.ds(0, tg)], buf.at[slot, k], sem.at[slot]).wait()
    acc = resid_ref[...]
    for k in range(TOP_K):
        acc = acc + gate_ref[:, k:k + 1] * buf[slot, k]
    y_o[...] = _layer_norm(acc, lng[...], lnb[...])


def _combine(resid, gate, dest, ys, lng, lnb):
    t = resid.shape[0]
    tg = min(COMBINE_TILE, t)
    nt = t // tg
    row = lambda w: pl.BlockSpec((tg, w), lambda i: (i, 0))
    return pl.pallas_call(
        functools.partial(_final_kernel, n=nt),
        grid=(nt,),
        in_specs=[row(D_MODEL), row(LANES),
                  pl.BlockSpec((tg * TOP_K,), lambda i: (i,), memory_space=pltpu.SMEM),
                  pl.BlockSpec((tg * TOP_K,), lambda i: (jnp.minimum(i + 1, nt - 1),), memory_space=pltpu.SMEM),
                  pl.BlockSpec(memory_space=pl.ANY),
                  _const_spec(lng.shape), _const_spec(lnb.shape)],
        out_specs=row(D_MODEL),
        out_shape=jax.ShapeDtypeStruct((t, D_MODEL), F32),
        scratch_shapes=[pltpu.VMEM((2, TOP_K, tg, ys.shape[1]), F32), pltpu.SemaphoreType.DMA((2,))],
        compiler_params=pltpu.CompilerParams(dimension_semantics=("arbitrary",), vmem_limit_bytes=VMEM_LIMIT),
        name="combine",
    )(resid, gate, dest, dest, ys, lng, lnb)


def _layer(x_prompt, x_sample, cache_k, cache_v, cache_logf, p_prompt, p_sample,
           w_in, b_in, lnv_g, lnv_b, w_s, b_s, w_pa, w_pb, w_o, ln1_g, ln1_b,
           w_router, b_router, w_gu, b_gu, w_dn, b_dn, w_ple_gate, b_ple_gate, w_ple_proj, ln2_g, ln2_b,
           dn_alpha):
    nb, seq, d = x_prompt.shape
    sb, new, _ = x_sample.shape
    past = cache_k.shape[1]
    tp, ts = nb * seq, sb * new
    vec = lambda a: a.reshape(1, -1).astype(F32)

    ws, bs = _proj_weights(w_in, b_in)
    lng, lnb = vec(lnv_g), vec(lnv_b)
    xp = x_prompt.reshape(tp, d)
    xsm = x_sample.reshape(ts, d)
    u_p, va_p, k_p, v_p, lf_p, g_p, qa_p, ka_p, vt_p = _project_prompt(xp, seq, ws, bs, lng, lnb)
    u_s, va_s, k_s, v_s, lf_s, g_s, q_s = _project_sample(xsm, ws, bs, lng, lnb)

    attn_p = _attend_prompt(qa_p, ka_p, vt_p, nb, seq)
    cache_lf = jnp.pad(cache_logf.astype(F32), ((0, 0), (0, 0), (0, LANES - N_HEADS)))
    attn_s = _attend_sample(q_s.reshape(sb, new, d), k_s.reshape(sb, new, d), v_s.reshape(sb, new, d),
                            lf_s.reshape(sb, new, LANES), cache_k, cache_v, cache_lf)

    wr = jnp.pad(w_router.astype(F32), ((0, 0), (0, LANES - N_EXPERTS)))
    wr_hi = wr.astype(BF16)
    wr_lo = (wr - wr_hi.astype(F32)).astype(BF16)
    br = jnp.concatenate([b_router.astype(F32), jnp.full((LANES - N_EXPERTS,), NEG_BIG, F32)]).reshape(1, LANES)
    spatial = b_s.shape[1]
    bst = jnp.pad(b_s.astype(F32).T, ((0, 0), (0, LANES - b_s.shape[0])))
    post_consts = [w_s.astype(F32), bst, w_pa.astype(BF16), w_pb.astype(BF16), w_o.astype(BF16),
                   vec(ln1_g), vec(ln1_b), wr_hi, wr_lo, br,
                   w_ple_gate.astype(BF16), vec(b_ple_gate), w_ple_proj.astype(BF16)]
    base0 = jnp.zeros((1, LANES), F32)
    x1r_p, resid_p, topi_p, gate_p, rank_p, cnt_p = _post_attention(
        xp, u_p, va_p, g_p, attn_p, p_prompt.reshape(tp, -1), base0, post_consts, min(spatial, seq), dn_alpha)
    x1r_s, resid_s, topi_s, gate_s, rank_s, cnt = _post_attention(
        xsm, u_s, va_s.astype(BF16), g_s, attn_s, p_sample.reshape(ts, -1), cnt_p, post_consts,
        min(spatial, new), dn_alpha)

    counts = cnt[0, :N_EXPERTS].astype(I32)
    padded = (counts + EXPERT_BLOCK - 1) // EXPERT_BLOCK * EXPERT_BLOCK
    pad_end = jnp.cumsum(padded)
    pad_start = (pad_end - padded).astype(I32)
    n_assign = (tp + ts) * TOP_K
    nblk = -(-n_assign // EXPERT_BLOCK) + N_EXPERTS
    n_rows = nblk * EXPERT_BLOCK
    block_row0 = jnp.arange(nblk, dtype=I32) * EXPERT_BLOCK
    block_expert = jnp.minimum(jnp.sum(pad_end[None, :] <= block_row0[:, None], axis=1), N_EXPERTS - 1).astype(I32)
    n_used = (pad_end[-1:] // EXPERT_BLOCK).astype(I32)

    dest_of = lambda topi, rank: (jnp.take(pad_start, topi[:, :TOP_K]) + rank[:, :TOP_K]).reshape(-1)
    dest_p, dest_s = dest_of(topi_p, rank_p), dest_of(topi_s, rank_s)
    xs = _dispatch(pad_start + counts, padded - counts, n_used, x1r_p, x1r_s,
                   jnp.concatenate([dest_p, dest_s]), n_rows)

    ys = _experts(block_expert, n_used, xs, w_gu.astype(F32), b_gu.astype(F32)[:, None, :],
                  w_dn.astype(F32), b_dn.astype(F32)[:, None, :])

    l2g, l2b = vec(ln2_g), vec(ln2_b)
    y_p = _combine(resid_p, gate_p, dest_p, ys, l2g, l2b)
    y_s = _combine(resid_s, gate_s, dest_s, ys, l2g, l2b)

    shp = lambda a, b_, n: a.reshape(b_, n, N_HEADS, HEAD_DIM)
    return (y_p.reshape(nb, seq, d), y_s.reshape(sb, new, d),
            shp(k_p, nb, seq), shp(v_p, nb, seq), lf_p.reshape(nb, seq, N_HEADS),
            shp(k_s, sb, new), shp(v_s, sb, new), lf_s[:, :N_HEADS].reshape(sb, new, N_HEADS),
            va_s.reshape(sb, new, N_GROUPS, d // N_GROUPS))


def kernel(x_prompt, x_sample, cache_fox_k, cache_fox_v, cache_fox_logf, p_prompt, p_sample, w_in, b_in, lnv_g, lnv_b, w_s, b_s, w_pa, w_pb, w_o, ln1_g, ln1_b, w_router, b_router, w_gu, b_gu, w_dn, b_dn, w_ple_gate, b_ple_gate, w_ple_proj, ln2_g, ln2_b):
    depth = w_in.shape[0]
    assert depth == 1, "the layer loop below carries one layer"
    dn_alpha = float((2 * depth) ** 0.25)
    outs = _layer(x_prompt, x_sample, cache_fox_k[0], cache_fox_v[0], cache_fox_logf[0], p_prompt[0], p_sample[0],
                  w_in[0], b_in[0], lnv_g[0], lnv_b[0], w_s[0], b_s[0], w_pa[0], w_pb[0], w_o[0], ln1_g[0], ln1_b[0],
                  w_router[0], b_router[0], w_gu[0], b_gu[0], w_dn[0], b_dn[0], w_ple_gate[0], b_ple_gate[0],
                  w_ple_proj[0], ln2_g[0], ln2_b[0], dn_alpha)
    y_p, y_s, k_p, v_p, lf_p, k_s, v_s, lf_s, va_s = outs
    lead = lambda a: a[None]
    return (y_p, y_s, lead(k_p), lead(v_p), lead(lf_p), lead(k_s), lead(v_s), lead(lf_s), lead(va_s))
```

```python
import functools

import jax
import jax.numpy as jnp
import numpy as np
from jax import lax
from jax.experimental import pallas as pl
from jax.experimental.pallas import tpu as pltpu

F32 = jnp.float32
BF16 = jnp.bfloat16
I32 = jnp.int32

LANES = 128
SUBLANES = 8
D_MODEL = 1024
N_HEADS = 16
HEAD_DIM = 64
N_GROUPS = 8
N_EXPERTS = 32
TOP_K = 4
SWIGLU_ALPHA = 1.702
SWIGLU_LIMIT = 7.0
LN_EPS = 1e-5
NEG_BIG = -1e30
LOG2_E = 1.4426950408889634
EXP_ROWS = 32
DEN_ROWS = 16

TOKEN_TILE = 256
Q_TILE = 512
SAMPLE_CHUNK = 512
EXPERT_BLOCK = 512
DISPATCH_TILE = 1024
COMBINE_TILE = 256
DMA_UNROLL = 8
VMEM_LIMIT = 56 * 1024 * 1024

F_MID_SHIFT = N_HEADS
F_LO_SHIFT = 2 * N_HEADS
F_ONE_LANE = 3 * N_HEADS


def _dot(a, b):
    return jnp.dot(a, b, preferred_element_type=F32)


def _dot_nt(a, b):
    return lax.dot_general(a, b, (((1,), (1,)), ((), ())), preferred_element_type=F32)


def _layer_norm(x, g, b):
    mu = jnp.mean(x, axis=-1, keepdims=True)
    xc = x - mu
    var = jnp.mean(xc * xc, axis=-1, keepdims=True)
    return xc * lax.rsqrt(var + LN_EPS) * g + b


def _log_sigmoid(z):
    return jnp.minimum(z, 0.0) - jnp.log1p(jnp.exp(-jnp.abs(z)))


def _split3(x):
    hi = x.astype(BF16).astype(F32)
    r = x - hi
    mid = r.astype(BF16).astype(F32)
    lo = (r - mid).astype(BF16).astype(F32)
    return hi, mid, lo


def _tri(n, strict):
    r = lax.broadcasted_iota(I32, (n, n), 0)
    c = lax.broadcasted_iota(I32, (n, n), 1)
    keep = (r > c) if strict else (r >= c)
    return jnp.where(keep, 1.0, 0.0).astype(F32)


def _cumsum_rows(x):
    n = x.shape[0]
    hi, mid, lo = _split3(x)
    cat = jnp.concatenate([hi, mid, lo], axis=1).astype(BF16)
    c = _dot(_tri(n, strict=False).astype(BF16), cat)
    return (c[:, 2 * LANES:] + c[:, LANES:2 * LANES]) + c[:, :LANES]


def _proj_sections(x_ref, wuv, wkv, wf, wg, buv, bkv, bf_, bg, lng, lnb, u_o, g_o):
    xb = x_ref[...].astype(BF16)
    pu = _dot(xb, wuv[:, :D_MODEL]) + buv[:, :D_MODEL]
    u_o[...] = jax.nn.gelu(pu).astype(BF16)
    pv = _dot(xb, wuv[:, D_MODEL:]) + buv[:, D_MODEL:]
    va = _layer_norm(jax.nn.gelu(pv), lng[...], lnb[...])
    k = _dot(xb, wkv[:, :D_MODEL]) + bkv[:, :D_MODEL]
    v = _dot(xb, wkv[:, D_MODEL:]) + bkv[:, D_MODEL:]
    g_o[...] = jax.nn.sigmoid(_dot(xb, wg[...]) + bg[...]).astype(BF16)
    f = _dot(xb, wf[...]) + bf_[...]
    lane = lax.broadcasted_iota(I32, f.shape, 1)
    lf = jnp.where(lane < N_HEADS, _log_sigmoid(f), 0.0)
    return xb, va, k, v, lf


def _proj_prompt_kernel(x_ref, wuv, wq, wkv, wf, wg, buv, bq, bkv, bf_, bg, lng, lnb, eq, ek,
                        u_o, va_o, k_o, v_o, lf_o, g_o, qa_o, ka_o, vt_o, carry, *, tiles_per_seq):
    i = pl.program_id(0)
    xb, va, k, v, lf = _proj_sections(x_ref, wuv, wkv, wf, wg, buv, bkv, bf_, bg, lng, lnb, u_o, g_o)
    va_o[...] = va.astype(BF16)
    v_t = v.T
    k_o[0] = k.T.reshape(k_o.shape[1:])
    v_o[0] = v_t.reshape(v_o.shape[1:])
    vt_o[0, 0] = v_t.astype(BF16)
    lf_o[...] = lf[:, :N_HEADS]

    @pl.when(i % tiles_per_seq == 0)
    def _():
        carry[...] = jnp.zeros_like(carry)

    tm = lf.shape[0]
    fcum = carry[...] + _cumsum_rows(lf)
    carry[...] = fcum[tm - 1:tm, :]

    hi, mid, lo = _split3(fcum * LOG2_E)
    lane = lax.broadcasted_iota(I32, fcum.shape, 1)
    pack = (hi + pltpu.roll(mid, F_MID_SHIFT, 1) + pltpu.roll(lo, F_LO_SHIFT, 1)
            + jnp.where(lane == F_ONE_LANE, 1.0, 0.0)).astype(BF16)
    add_q = _dot(pack, eq[...])
    add_k = _dot(pack, ek[...])
    q = (_dot(xb, wq[...]) + bq[...]) * (HEAD_DIM ** -0.5 * LOG2_E)
    low =lane < HEAD_DIM
    for g in range(N_GROUPS):
        sl = slice(g * LANES, (g + 1) * LANES)
        ev = slice(2 * g * LANES, (2 * g + 1) * LANES)
        od = slice((2 * g + 1) * LANES, (2 * g + 2) * LANES)
        qa_o[:, ev] = jnp.where(low, q[:, sl], add_q[:, ev]).astype(BF16)
        qa_o[:, od] = jnp.where(low, add_q[:, od], q[:, sl]).astype(BF16)
        ka_o[:, ev] = jnp.where(low, k[:, sl], add_k[:, ev]).astype(BF16)
        ka_o[:, od] = jnp.where(low, add_k[:, od], k[:, sl]).astype(BF16)


def _proj_sample_kernel(x_ref, wuv, wq, wkv, wf, wg, buv, bq, bkv, bf_, bg, lng, lnb,
                        u_o, va_o, k_o, v_o, lf_o, g_o, q_o):
    xb, va, k, v, lf = _proj_sections(x_ref, wuv, wkv, wf, wg, buv, bkv, bf_, bg, lng, lnb, u_o, g_o)
    k_o[...] = k
    v_o[...] = v
    va_o[...] = va
    lf_o[...] = lf
    q_o[...] = (_dot(xb, wq[...]) + bq[...]) * (HEAD_DIM ** -0.5)


def _const_spec(shape):
    nd = len(shape)
    return pl.BlockSpec(shape, lambda *_: (0,) * nd, pipeline_mode=pl.Buffered(1))


def _proj_weights(w_in, b_in):
    d = D_MODEL
    off_q, off_k, off_f = 2 * d, 3 * d, 5 * d
    off_ga = off_f + N_HEADS
    wuv = w_in[:, :off_q].astype(BF16)
    wq = w_in[:, off_q:off_k].astype(BF16)
    wkv = w_in[:, off_k:off_f].astype(BF16)
    wf = jnp.pad(w_in[:, off_f:off_ga], ((0, 0), (0, LANES - N_HEADS))).astype(BF16)
    wg = w_in[:, off_ga:].astype(BF16)
    b = b_in.reshape(1, -1).astype(F32)
    buv, bq, bkv = b[:, :off_q], b[:, off_q:off_k], b[:, off_k:off_f]
    bf_ = jnp.pad(b[:, off_f:off_ga], ((0, 0), (0, LANES - N_HEADS)))
    bg = b[:, off_ga:]
    return (wuv, wq, wkv, wf, wg), (buv, bq, bkv, bf_, bg)


def _spread_matrices():
    eq_np = np.zeros((LANES, 2 * D_MODEL), np.float32)
    ek_np = np.zeros((LANES, 2 * D_MODEL), np.float32)
    for h in range(N_HEADS):
        base = LANES * h + (HEAD_DIM if h % 2 == 0 else 0)
        for c, shift in enumerate((0, F_MID_SHIFT, F_LO_SHIFT)):
            eq_np[shift + h, base + c] = 1.0
            eq_np[F_ONE_LANE, base + 3 + c] = 1.0
            ek_np[F_ONE_LANE, base + c] = 1.0
            ek_np[shift + h, base + 3 + c] = -1.0
    return jnp.asarray(eq_np, BF16), jnp.asarray(ek_np, BF16)


def _project_prompt(x2d, seq, ws, bs, lng, lnb):
    t = x2d.shape[0]
    tm = min(TOKEN_TILE, seq)
    nt = t // tm
    nb = t // seq
    eq, ek = _spread_matrices()
    consts = list(ws) + list(bs) + [lng, lnb, eq, ek]
    row = lambda w: pl.BlockSpec((tm, w), lambda i: (i, 0))
    out_shape = (
        jax.ShapeDtypeStruct((t, D_MODEL), BF16),
        jax.ShapeDtypeStruct((t, D_MODEL), BF16),
        jax.ShapeDtypeStruct((nb, N_HEADS, HEAD_DIM, seq), F32),
        jax.ShapeDtypeStruct((nb, N_HEADS, HEAD_DIM, seq), F32),
        jax.ShapeDtypeStruct((t, N_HEADS), F32),
        jax.ShapeDtypeStruct((t, 2 * D_MODEL), BF16),
        jax.ShapeDtypeStruct((t, 2 * D_MODEL), BF16),
        jax.ShapeDtypeStruct((t, 2 * D_MODEL), BF16),
        jax.ShapeDtypeStruct((nb, seq // tm, D_MODEL, tm), BF16),
    )
    tiles_per_seq = seq // tm
    heads = pl.BlockSpec((1, N_HEADS, HEAD_DIM, tm), lambda i: (i // tiles_per_seq, 0, 0, i % tiles_per_seq))
    out_specs = (row(D_MODEL), row(D_MODEL), heads, heads, row(N_HEADS),
                 row(2 * D_MODEL), row(2 * D_MODEL), row(2 * D_MODEL),
                 pl.BlockSpec((1, 1, D_MODEL, tm), lambda i: (i // tiles_per_seq, i % tiles_per_seq, 0, 0)))
    return pl.pallas_call(
        functools.partial(_proj_prompt_kernel, tiles_per_seq=tiles_per_seq),
        grid=(nt,),
        in_specs=[row(D_MODEL)] + [_const_spec(c.shape) for c in consts],
        out_specs=out_specs,
        out_shape=out_shape,
        scratch_shapes=[pltpu.VMEM((1, LANES), F32)],
        compiler_params=pltpu.CompilerParams(dimension_semantics=("arbitrary",), vmem_limit_bytes=VMEM_LIMIT),
        name="proj_prompt",
    )(x2d, *consts)


def _project_sample(x2d, ws, bs, lng, lnb):
    t = x2d.shape[0]
    tm = min(TOKEN_TILE, t)
    nt = t // tm
    consts = list(ws) + list(bs) + [lng, lnb]
    row = lambda w: pl.BlockSpec((tm, w), lambda i: (i, 0))
    out_shape = (
        jax.ShapeDtypeStruct((t, D_MODEL), BF16),
        jax.ShapeDtypeStruct((t, D_MODEL), F32),
        jax.ShapeDtypeStruct((t, D_MODEL), F32),
        jax.ShapeDtypeStruct((t, D_MODEL), F32),
        jax.ShapeDtypeStruct((t, LANES), F32),
        jax.ShapeDtypeStruct((t, 2 * D_MODEL), BF16),
        jax.ShapeDtypeStruct((t, D_MODEL), F32),
    )
    out_specs = (row(D_MODEL), row(D_MODEL), row(D_MODEL), row(D_MODEL), row(LANES), row(2 * D_MODEL), row(D_MODEL))
    return pl.pallas_call(
        _proj_sample_kernel,
        grid=(nt,),
        in_specs=[row(D_MODEL)] + [_const_spec(c.shape) for c in consts],
        out_specs=out_specs,
        out_shape=out_shape,
        compiler_params=pltpu.CompilerParams(dimension_semantics=("arbitrary",), vmem_limit_bytes=VMEM_LIMIT),
        name="proj_sample",
    )(x2d, *consts)


def _attn_prompt_kernel(q_ref, k_ref, vt_ref, o_ref, m_ref, acc_ref, ot_ref,
                        s0, s1, p0, p1, a0, a1, c0, c1, *, tq, tk):
    qi = pl.program_id(2)
    assert tq == 2 * tk, "the pipeline below is written for two diagonal spans per query tile"
    m_ref[...] = jnp.full(m_ref.shape, NEG_BIG, F32)
    acc_ref[...] = jnp.zeros(acc_ref.shape, F32)
    p1[...] = jnp.zeros(p1.shape, BF16)
    a1[...] = jnp.ones(a1.shape, F32)
    tv = vt_ref.shape[3]
    ones = jnp.ones((DEN_ROWS, tv), BF16)
    lanes = lambda hh: slice(hh * LANES, (hh + 1) * LANES)

    def score(t, s_slot, c_slot):
        start = pl.multiple_of(t * tk, tk)
        for hh in range(2):
            st = _dot_nt(k_ref[pl.ds(start, tk), lanes(hh)], q_ref[:, lanes(hh)])
            s_slot[hh] = st
            c_slot[hh] = jnp.max(st, axis=0, keepdims=True)

    def softmax(s_slot, c_slot, p_slot, a_slot, mask_offset):
        for hh in range(2):
            if mask_offset is None:
                col_max = c_slot[hh]
            else:
                key_row = lax.broadcasted_iota(I32, (tk, tq), 0)
                qry_col = lax.broadcasted_iota(I32, (tk, tq), 1)
                s_slot[hh] = jnp.where(key_row + mask_offset <= qry_col, s_slot[hh], NEG_BIG)
                col_max = jnp.max(s_slot[hh], axis=0, keepdims=True)
            m_old = m_ref[hh]
            m_new = jnp.maximum(m_old, col_max)
            a_slot[hh] = jnp.exp2(m_old - m_new)
            for r in range(0, tk, EXP_ROWS):
                rows = slice(r, r + EXP_ROWS)
                p_slot[hh, rows, :] = jnp.exp2((s_slot[hh, rows, :] - m_new).astype(BF16))
            m_ref[hh] = m_new

    def value(t, p_slot, a_slot):
        span = jnp.maximum(t, 0)
        for hh in range(2):
            pv = None
            for d in range(tk // tv):
                vt = vt_ref[0, span * (tk // tv) + d, hh * HEAD_DIM:(hh + 1) * HEAD_DIM, :]
                part = _dot(jnp.concatenate([vt, ones], axis=0), p_slot[hh, d * tv:(d + 1) * tv, :])
                pv = part if pv is None else pv + part
            acc_ref[hh] = a_slot[hh] * acc_ref[hh] + pv

    score(0, s0, c0)

    def two_steps(ii):
        t = 2 * ii + 1
        value(t - 2, p1, a1)
        score(t, s1, c1)
        softmax(s0, c0, p0, a0, None)
        value(t - 1, p0, a0)
        score(t + 1, s0, c0)
        softmax(s1, c1, p1, a1, None)

    def body(jj, c):
        two_steps(2 * jj)
        two_steps(2 * jj + 1)
        return c

    lax.fori_loop(0, qi // 2, body, 0)

    @pl.when(qi % 2 == 1)
    def _():
        two_steps(qi - 1)

    t = 2 * qi + 1
    value(t - 2, p1, a1)
    score(t, s1, c1)
    softmax(s0, c0, p0, a0, 0)
    value(t - 1, p0, a0)
    softmax(s1, c1, p1, a1, tk)
    value(t, p1, a1)
    for hh in range(2):
        ot_ref[hh * HEAD_DIM:(hh + 1) * HEAD_DIM, :] = acc_ref[hh, :HEAD_DIM] / acc_ref[hh, HEAD_DIM:HEAD_DIM + 1]
    o_ref[...] = ot_ref[...].T.astype(BF16)


def _attend_prompt(q_aug, k_aug, v_t, nb, seq):
    t = q_aug.shape[0]
    tv = v_t.shape[3]
    tq = min(Q_TILE, seq)
    tk = tq // 2
    nq = seq // tq
    return pl.pallas_call(
        functools.partial(_attn_prompt_kernel, tq=tq, tk=tk),
        grid=(nb, N_GROUPS, nq),
        in_specs=[
            pl.BlockSpec((tq, 2 * LANES), lambda b, g, i: (b * nq + i, g)),
            pl.BlockSpec((seq, 2 * LANES), lambda b, g, i: (b, g)),
            pl.BlockSpec((1, seq // tv, LANES, tv), lambda b, g, i: (b, 0, g, 0)),
        ],
        out_specs=pl.BlockSpec((tq, LANES), lambda b, g, i: (b * nq + i, g)),
        out_shape=jax.ShapeDtypeStruct((t, D_MODEL), BF16),
        scratch_shapes=[pltpu.VMEM((2, 1, tq), F32), pltpu.VMEM((2, HEAD_DIM + DEN_ROWS, tq), F32),
                        pltpu.VMEM((LANES, tq), F32),
                        pltpu.VMEM((2, tk, tq), F32), pltpu.VMEM((2, tk, tq), F32),
                        pltpu.VMEM((2, tk, tq), BF16), pltpu.VMEM((2, tk, tq), BF16),
                        pltpu.VMEM((2, 1, tq), F32), pltpu.VMEM((2, 1, tq), F32),
                        pltpu.VMEM((2, 1, tq), F32), pltpu.VMEM((2, 1, tq), F32)],
        compiler_params=pltpu.CompilerParams(
            dimension_semantics=("arbitrary", "arbitrary", "arbitrary"), vmem_limit_bytes=VMEM_LIMIT),
        name="attn_prompt",
    )(q_aug, k_aug, v_t)


def _attn_sample_kernel(q_ref, k_ref, v_ref, lf_ref, ck_ref, cv_ref, clf_ref, o_ref,
                        fct_sc, fn_sc, m_sc, l_sc, acc_sc, *, n_chunks):
    c = pl.program_id(1)
    chunk = ck_ref.shape[3]
    new = q_ref.shape[1]

    @pl.when(c == 0)
    def _():
        carry = jnp.zeros((1, LANES), F32)
        for cc in range(n_chunks):
            fc = carry + _cumsum_rows(clf_ref[0, cc * chunk:(cc + 1) * chunk, :])
            carry = fc[chunk - 1:chunk, :]
            fct_sc[cc] = fc.T
        fn_sc[...] = carry + _cumsum_rows(lf_ref[0])
        m_sc[...] = jnp.full(m_sc.shape, NEG_BIG, F32)
        l_sc[...] = jnp.zeros(l_sc.shape, F32)
        acc_sc[...] = jnp.zeros(acc_sc.shape, F32)

    lane = lax.broadcasted_iota(I32, (new, LANES), 1)
    f_new = fn_sc[...]

    def attend(score, weigh, f_keys_t, mask):
        for g in range(N_GROUPS):
            qg = q_ref[0, :, g * LANES:(g + 1) * LANES]
            for hh in range(2):
                h = 2 * g + hh
                mine = (lane < HEAD_DIM) if hh == 0 else (lane >= HEAD_DIM)
                qm = jnp.where(mine, qg, 0.0).astype(BF16)
                s = score(qm, g) + f_new[:, h:h + 1] - f_keys_t[h:h + 1, :]
                if mask is not None:
                    s = jnp.where(mask, s, NEG_BIG)
                m_old = m_sc[h]
                m_new = jnp.maximum(m_old, jnp.max(s, axis=1, keepdims=True))
                alpha = jnp.exp(m_old - m_new)
                p = jnp.exp(s - m_new)
                l_sc[h] = alpha * l_sc[h] + jnp.sum(p, axis=1, keepdims=True)
                acc_sc[h] = alpha * acc_sc[h] + weigh(p.astype(BF16), g)
                m_sc[h] = m_new

    pair_t = lambda ref, g: ref[0, 2 * g:2 * g + 2].reshape(LANES, chunk).astype(BF16)
    attend(lambda qm, g: _dot(qm, pair_t(ck_ref, g)), lambda p, g: _dot_nt(p, pair_t(cv_ref, g)), fct_sc[c], None)

    @pl.when(c == n_chunks - 1)
    def _():
        causal = lax.broadcasted_iota(I32, (new, new), 1) <= lax.broadcasted_iota(I32, (new, new), 0)
        cols = lambda ref, g: ref[0, :, g * LANES:(g + 1) * LANES].astype(BF16)
        attend(lambda qm, g: _dot_nt(qm, cols(k_ref, g)), lambda p, g: _dot(p, cols(v_ref, g)), f_new.T, causal)
        for g in range(N_GROUPS):
            even = acc_sc[2 * g] / l_sc[2 * g]
            odd = acc_sc[2 * g + 1] / l_sc[2 * g + 1]
            o_ref[:, g * LANES:(g + 1) * LANES] = jnp.where(lane < HEAD_DIM, even, odd).astype(BF16)


def _attend_sample(q, k, v, lf, cache_k, cache_v, cache_lf):
    nb, new, _ = q.shape
    past = cache_k.shape[3]
    chunk = min(SAMPLE_CHUNK, past)
    n_chunks = past // chunk
    blk = lambda n, w: pl.BlockSpec((1, n, w), lambda b, c: (b, 0, 0))
    cache = pl.BlockSpec((1, N_HEADS, HEAD_DIM, chunk), lambda b, c: (b, 0, 0, c))
    return pl.pallas_call(
        functools.partial(_attn_sample_kernel, n_chunks=n_chunks),
        grid=(nb, n_chunks),
        in_specs=[blk(new, D_MODEL), blk(new, D_MODEL), blk(new, D_MODEL), blk(new, LANES),
                  cache, cache, blk(past, LANES)],
        out_specs=pl.BlockSpec((new, D_MODEL), lambda b, c: (b, 0)),
        out_shape=jax.ShapeDtypeStruct((nb * new, D_MODEL), BF16),
        scratch_shapes=[pltpu.VMEM((n_chunks, LANES, chunk), F32), pltpu.VMEM((new, LANES), F32),
                        pltpu.VMEM((N_HEADS, new, 1), F32), pltpu.VMEM((N_HEADS, new, 1), F32),
                        pltpu.VMEM((N_HEADS, new, LANES), F32)],
        compiler_params=pltpu.CompilerParams(dimension_semantics=("arbitrary", "arbitrary"),
                                             vmem_limit_bytes=VMEM_LIMIT),
        name="attn_sample",
    )(q, k, v, lf, cache_k, cache_v, cache_lf)


def _post_kernel(x_ref, u_ref, va_ref, g_ref, at_ref, p_ref, base_ref, ws_ref, bst_ref, wpa, wpb, wo,
                 l1g, l1b, wr_hi, wr_lo, br, wpg, bpg, wpp,
                 x1_o, resid_o, topi_o, gate_o, rank_o, cnt_o, a_sc, base_sc, *, chunk, dn_alpha):
    i = pl.program_id(0)
    tm = x_ref.shape[0]

    @pl.when(i == 0)
    def _():
        base_sc[...] = base_ref[...]

    tri = _tri(chunk, strict=False)
    for g in range(N_GROUPS):
        sl = slice(g * LANES, (g + 1) * LANES)
        wm = (ws_ref[g, :chunk, :chunk] * tri).astype(BF16)
        bias = bst_ref[:chunk, g:g + 1]
        for c in range(tm // chunk):
            rows = slice(c * chunk, (c + 1) * chunk)
            sg = _dot(wm, va_ref[rows, sl]) + bias
            a_sc[rows, sl] = (u_ref[rows, sl].astype(F32) * sg).astype(BF16)

    pa = _dot(a_sc[...], wpa[...])
    pb = _dot(at_ref[...], wpb[...])
    merged = g_ref[:, :D_MODEL].astype(F32) * pa + g_ref[:, D_MODEL:].astype(F32) * pb
    x1 = _layer_norm(dn_alpha * x_ref[...] + _dot(merged.astype(BF16), wo[...]), l1g[...], l1b[...])
    x1b = x1.astype(BF16)
    x1_o[...] = x1

    ple = jax.nn.sigmoid(_dot(x1b, wpg[...]) + bpg[...]) * _dot(p_ref[...].astype(BF16), wpp[...])
    resid_o[...] = dn_alpha * x1 + ple

    x1_lo = (x1 - x1b.astype(F32)).astype(BF16)
    logits = (_dot(x1b, wr_hi[...]) + (_dot(x1b, wr_lo[...]) + _dot(x1_lo, wr_hi[...]))) + br[...]
    lane = lax.broadcasted_iota(I32, (tm, LANES), 1)
    work = logits
    vals, idxs = [], []
    for _ in range(TOP_K):
        mk = jnp.max(work, axis=1, keepdims=True)
        ik = jnp.min(jnp.where(work == mk, lane, LANES), axis=1, keepdims=True)
        vals.append(mk)
        idxs.append(ik)
        work = jnp.where(lane == ik, -jnp.inf, work)
    exps = [jnp.exp(v - vals[0]) for v in vals]
    den = exps[0] + exps[1] + exps[2] + exps[3]

    lower = _tri(tm, strict=True).astype(BF16)
    run = base_sc[...]
    topi = jnp.zeros((tm, LANES), I32)
    gate = jnp.zeros((tm, LANES), F32)
    rank = jnp.zeros((tm, LANES), F32)
    for k in range(TOP_K):
        onehot = jnp.where(lane == idxs[k], 1.0, 0.0)
        before = _dot(lower, onehot.astype(BF16))
        rk = jnp.sum(onehot * (before + run), axis=1, keepdims=True)
        topi = jnp.where(lane == k, idxs[k], topi)
        gate = jnp.where(lane == k, exps[k] / den, gate)
        rank = jnp.where(lane == k, rk, rank)
        run = run + jnp.sum(onehot, axis=0, keepdims=True)
    base_sc[...] = run
    topi_o[...] = topi
    gate_o[...] = gate
    rank_o[...] = rank.astype(I32)
    cnt_o[...] = run


def _post_attention(x2d, u, va, gates, attn, p2d, base, consts, chunk, dn_alpha):
    t = x2d.shape[0]
    tm = min(TOKEN_TILE, t)
    nt = t // tm
    row = lambda w: pl.BlockSpec((tm, w), lambda i: (i, 0))
    ple_dim = p2d.shape[1]
    out_shape = (
        jax.ShapeDtypeStruct((t, D_MODEL), F32),
        jax.ShapeDtypeStruct((t, D_MODEL), F32),
        jax.ShapeDtypeStruct((t, LANES), I32),
        jax.ShapeDtypeStruct((t, LANES), F32),
        jax.ShapeDtypeStruct((t, LANES), I32),
        jax.ShapeDtypeStruct((1, LANES), F32),
    )
    out_specs = (row(D_MODEL), row(D_MODEL), row(LANES), row(LANES), row(LANES),
                 pl.BlockSpec((1, LANES), lambda i: (0, 0)))
    return pl.pallas_call(
        functools.partial(_post_kernel, chunk=chunk, dn_alpha=dn_alpha),
        grid=(nt,),
        in_specs=[row(D_MODEL), row(D_MODEL), row(D_MODEL), row(2 * D_MODEL), row(D_MODEL), row(ple_dim),
                  _const_spec(base.shape)] + [_const_spec(c.shape) for c in consts],
        out_specs=out_specs,
        out_shape=out_shape,
        scratch_shapes=[pltpu.VMEM((tm, D_MODEL), BF16), pltpu.VMEM((1, LANES), F32)],
        compiler_params=pltpu.CompilerParams(dimension_semantics=("arbitrary",), vmem_limit_bytes=VMEM_LIMIT),
        name="post_attention",
    )(x2d, u, va, gates, attn, p2d, base, *consts)


def _dispatch_kernel(fill_start_ref, fill_len_ref, nu_ref, xp_ref, xn_ref, dest_ref, xs_out, zero_sc, sem, zsem,
                     *, prompt_steps, n_blocks):
    i = pl.program_id(0)
    td = xp_ref.shape[0]

    @pl.when(i == 0)
    def _():
        zero_sc[...] = jnp.zeros(zero_sc.shape, zero_sc.dtype)

        def zero_fill(wait):
            def go(copy):
                copy.wait() if wait else copy.start()

            def per_expert(e, c):
                n = fill_len_ref[e]
                base = fill_start_ref[e]
                head = jnp.bitwise_and(n, SUBLANES - 1)
                for r in range(SUBLANES - 1):
                    @pl.when(r < head)
                    def _(r=r):
                        go(pltpu.make_async_copy(zero_sc.at[pl.ds(0, 1)], xs_out.at[pl.ds(base + r, 1)], zsem))

                piece = EXPERT_BLOCK // 2
                while piece >= SUBLANES:
                    offset = pl.multiple_of(base + head + jnp.bitwise_and(n, -(2 * piece)), SUBLANES)

                    @pl.when(jnp.bitwise_and(n, piece) != 0)
                    def _(piece=piece, offset=offset):
                        go(pltpu.make_async_copy(zero_sc.at[pl.ds(0, piece)], xs_out.at[pl.ds(offset, piece)], zsem))

                    piece //= 2
                return c

            lax.fori_loop(0, N_EXPERTS, per_expert, 0)

            def per_block(j, c):
                row0 = pl.multiple_of(j * EXPERT_BLOCK, EXPERT_BLOCK)
                go(pltpu.make_async_copy(zero_sc, xs_out.at[pl.ds(row0, EXPERT_BLOCK)], zsem))
                return c

            lax.fori_loop(nu_ref[0], n_blocks, per_block, 0)

        zero_fill(wait=False)
        zero_fill(wait=True)

    def scatter(x_ref):
        def body(j, c):
            for u in range(DMA_UNROLL):
                tok = j * (DMA_UNROLL // TOP_K) + u // TOP_K
                pltpu.make_async_copy(x_ref.at[pl.ds(tok, 1)], xs_out.at[pl.ds(dest_ref[j * DMA_UNROLL + u], 1)],
                                      sem).start(priority=u % 2)
            return c

        lax.fori_loop(0, td * TOP_K // DMA_UNROLL, body, 0)
        for _ in range(TOP_K):
            pltpu.make_async_copy(x_ref, xs_out.at[pl.ds(0, td)], sem).wait()

    @pl.when(i < prompt_steps)
    def _():
        scatter(xp_ref)

    @pl.when(i >= prompt_steps)
    def _():
        scatter(xn_ref)


def _dispatch(fill_start, fill_len, n_used, x1_prompt, x1_sample, dest, n_rows):
    tp, w = x1_prompt.shape
    ts = x1_sample.shape[0]
    td = min(DISPATCH_TILE, ts)
    prompt_steps, sample_steps = tp // td, ts // td
    grid_spec = pltpu.PrefetchScalarGridSpec(
        num_scalar_prefetch=3,
        grid=(prompt_steps + sample_steps,),
        in_specs=[pl.BlockSpec((td, w), lambda i, *_: (jnp.minimum(i, prompt_steps - 1), 0)),
                  pl.BlockSpec((td, w), lambda i, *_: (jnp.maximum(i - prompt_steps, 0), 0)),
                  pl.BlockSpec((td * TOP_K,), lambda i, *_: (i,), memory_space=pltpu.SMEM)],
        out_specs=pl.BlockSpec(memory_space=pl.ANY),
        scratch_shapes=[pltpu.VMEM((EXPERT_BLOCK, w), x1_prompt.dtype),
                        pltpu.SemaphoreType.DMA(()), pltpu.SemaphoreType.DMA(())],
    )
    return pl.pallas_call(
        functools.partial(_dispatch_kernel, prompt_steps=prompt_steps, n_blocks=n_rows // EXPERT_BLOCK),
        grid_spec=grid_spec,
        out_shape=jax.ShapeDtypeStruct((n_rows, w), x1_prompt.dtype),
        compiler_params=pltpu.CompilerParams(dimension_semantics=("arbitrary",), vmem_limit_bytes=VMEM_LIMIT),
        name="dispatch",
    )(fill_start, fill_len, n_used, x1_prompt, x1_sample, dest)


def _expert_kernel(be_ref, nu_ref, xs_ref, wgu, bgu, wdn, bdn, ys_o, wgu_bf, wdn_bf):
    j = pl.program_id(0)
    used = j < nu_ref[0]
    new_expert = jnp.logical_or(j == 0, be_ref[j] != be_ref[jnp.maximum(j - 1, 0)])

    @pl.when(jnp.logical_not(used))
    def _():
        ys_o[...] = jnp.zeros(ys_o.shape, ys_o.dtype)

    @pl.when(jnp.logical_and(used, new_expert))
    def _():
        wgu_bf[...] = wgu[0].astype(BF16)
        wdn_bf[...] = wdn[0].astype(BF16)

    @pl.when(used)
    def _():
        h = _dot(xs_ref[...].astype(BF16), wgu_bf[...]) + bgu[0]
        d_e = h.shape[1] // 2
        gate = jnp.minimum(h[:, :d_e], SWIGLU_LIMIT)
        up = jnp.clip(h[:, d_e:], -SWIGLU_LIMIT, SWIGLU_LIMIT)
        glu = gate * jax.nn.sigmoid(gate * SWIGLU_ALPHA)
        ys_o[...] = _dot(((up + 1.0) * glu).astype(BF16), wdn_bf[...]) + bdn[0]


def _experts(block_expert, n_used, xs, wgu, bgu, wdn, bdn):
    rows, w = xs.shape
    nblk = rows // EXPERT_BLOCK
    d_in, d_gu = wgu.shape[1], wgu.shape[2]
    d_e, d_out = wdn.shape[1], wdn.shape[2]
    grid_spec = pltpu.PrefetchScalarGridSpec(
        num_scalar_prefetch=2,
        grid=(nblk,),
        in_specs=[
            pl.BlockSpec((EXPERT_BLOCK, w), lambda j, be, nu: (j, 0)),
            pl.BlockSpec((1, d_in, d_gu), lambda j, be, nu: (be[j], 0, 0)),
            pl.BlockSpec((1, 1, d_gu), lambda j, be, nu: (be[j], 0, 0)),
            pl.BlockSpec((1, d_e, d_out), lambda j, be, nu: (be[j], 0, 0)),
            pl.BlockSpec((1, 1, d_out), lambda j, be, nu: (be[j], 0, 0)),
        ],
        out_specs=pl.BlockSpec((EXPERT_BLOCK, d_out), lambda j, be, nu: (j, 0)),
        scratch_shapes=[pltpu.VMEM((d_in, d_gu), BF16), pltpu.VMEM((d_e, d_out), BF16)],
    )
    return pl.pallas_call(
        _expert_kernel,
        grid_spec=grid_spec,
        out_shape=jax.ShapeDtypeStruct((rows, d_out), F32),
        compiler_params=pltpu.CompilerParams(dimension_semantics=("arbitrary",), vmem_limit_bytes=VMEM_LIMIT),
        name="experts",
    )(block_expert, n_used, xs, wgu, bgu, wdn, bdn)


def _final_kernel(resid_ref, gate_ref, dest_ref, dest_next_ref, ys_ref, lng, lnb, y_o, buf, sem, *, n):
    i = pl.program_id(0)
    tg = resid_ref.shape[0]

    def gather(idx_ref, slot):
        def body(j, c):
            for u in range(DMA_UNROLL):
                tok = j * (DMA_UNROLL // TOP_K) + u // TOP_K
                pltpu.make_async_copy(ys_ref.at[pl.ds(idx_ref[j * DMA_UNROLL + u], 1)],
                                      buf.at[slot, u % TOP_K, pl.ds(tok, 1)], sem.at[slot]).start(priority=u % 2)
            return c

        lax.fori_loop(0, tg * TOP_K // DMA_UNROLL, body, 0)

    @pl.when(i == 0)
    def _():
        gather(dest_ref, 0)

    @pl.when(i + 1 < n)
    def _():
        gather(dest_next_ref, (i + 1) % 2)

    slot = i % 2
    for k in range(TOP_K):
        pltpu.make_async_copy(ys_ref.at[pl.ds(0, tg)], buf.at[slot, k], sem.at[slot]).wait()
    acc = resid_ref[...]
    for k in range(TOP_K):
        acc = acc + gate_ref[:, k:k + 1] * buf[slot, k]
    y_o[...] = _layer_norm(acc, lng[...], lnb[...])


def _combine(resid, gate, dest, ys, lng, lnb):
    t = resid.shape[0]
    tg = min(COMBINE_TILE, t)
    nt = t // tg
    row = lambda w: pl.BlockSpec((tg, w), lambda i: (i, 0))
    return pl.pallas_call(
        functools.partial(_final_kernel, n=nt),
        grid=(nt,),
        in_specs=[row(D_MODEL), row(LANES),
                  pl.BlockSpec((tg * TOP_K,), lambda i: (i,), memory_space=pltpu.SMEM),
                  pl.BlockSpec((tg * TOP_K,), lambda i: (jnp.minimum(i + 1, nt - 1),), memory_space=pltpu.SMEM),
                  pl.BlockSpec(memory_space=pl.ANY),
                  _const_spec(lng.shape), _const_spec(lnb.shape)],
        out_specs=row(D_MODEL),
        out_shape=jax.ShapeDtypeStruct((t, D_MODEL), F32),
        scratch_shapes=[pltpu.VMEM((2, TOP_K, tg, ys.shape[1]), F32), pltpu.SemaphoreType.DMA((2,))],
        compiler_params=pltpu.CompilerParams(dimension_semantics=("arbitrary",), vmem_limit_bytes=VMEM_LIMIT),
        name="combine",
    )(resid, gate, dest, dest, ys, lng, lnb)


def _layer(x_prompt, x_sample, cache_k, cache_v, cache_logf, p_prompt, p_sample,
           w_in, b_in, lnv_g, lnv_b, w_s, b_s, w_pa, w_pb, w_o, ln1_g, ln1_b,
           w_router, b_router, w_gu, b_gu, w_dn, b_dn, w_ple_gate, b_ple_gate, w_ple_proj, ln2_g, ln2_b,
           dn_alpha):
    nb, seq, d = x_prompt.shape
    sb, new, _ = x_sample.shape
    past = cache_k.shape[1]
    tp, ts = nb * seq, sb * new
    vec = lambda a: a.reshape(1, -1).astype(F32)

    ws, bs = _proj_weights(w_in, b_in)
    lng, lnb = vec(lnv_g), vec(lnv_b)
    xp = x_prompt.reshape(tp, d)
    xsm = x_sample.reshape(ts, d)
    u_p, va_p, k_p, v_p, lf_p, g_p, qa_p, ka_p, vt_p = _project_prompt(xp, seq, ws, bs, lng, lnb)
    u_s, va_s, k_s, v_s, lf_s, g_s, q_s = _project_sample(xsm, ws, bs, lng, lnb)

    attn_p = _attend_prompt(qa_p, ka_p, vt_p, nb, seq)
    cache_lf = jnp.pad(cache_logf.astype(F32), ((0, 0), (0, 0), (0, LANES - N_HEADS)))
    rows_last = lambda a: jnp.transpose(a, (0, 2, 3, 1))
    attn_s = _attend_sample(q_s.reshape(sb, new, d), k_s.reshape(sb, new, d), v_s.reshape(sb, new, d),
                            lf_s.reshape(sb, new, LANES), rows_last(cache_k), rows_last(cache_v), cache_lf)

    wr = jnp.pad(w_router.astype(F32), ((0, 0), (0, LANES - N_EXPERTS)))
    wr_hi = wr.astype(BF16)
    wr_lo = (wr - wr_hi.astype(F32)).astype(BF16)
    br = jnp.concatenate([b_router.astype(F32), jnp.full((LANES - N_EXPERTS,), NEG_BIG, F32)]).reshape(1, LANES)
    spatial = b_s.shape[1]
    bst = jnp.pad(b_s.astype(F32).T, ((0, 0), (0, LANES - b_s.shape[0])))
    post_consts = [w_s.astype(F32), bst, w_pa.astype(BF16), w_pb.astype(BF16), w_o.astype(BF16),
                   vec(ln1_g), vec(ln1_b), wr_hi, wr_lo, br,
                   w_ple_gate.astype(BF16), vec(b_ple_gate), w_ple_proj.astype(BF16)]
    base0 = jnp.zeros((1, LANES), F32)
    x1r_p, resid_p, topi_p, gate_p, rank_p, cnt_p = _post_attention(
        xp, u_p, va_p, g_p, attn_p, p_prompt.reshape(tp, -1), base0, post_consts, min(spatial, seq), dn_alpha)
    x1r_s, resid_s, topi_s, gate_s, rank_s, cnt = _post_attention(
        xsm, u_s, va_s.astype(BF16), g_s, attn_s, p_sample.reshape(ts, -1), cnt_p, post_consts,
        min(spatial, new), dn_alpha)

    counts = cnt[0, :N_EXPERTS].astype(I32)
    padded = (counts + EXPERT_BLOCK - 1) // EXPERT_BLOCK * EXPERT_BLOCK
    pad_end = jnp.cumsum(padded)
    pad_start = (pad_end - padded).astype(I32)
    n_assign = (tp + ts) * TOP_K
    nblk = -(-n_assign // EXPERT_BLOCK) + N_EXPERTS
    n_rows = nblk * EXPERT_BLOCK
    block_row0 = jnp.arange(nblk, dtype=I32) * EXPERT_BLOCK
    block_expert = jnp.minimum(jnp.sum(pad_end[None, :] <= block_row0[:, None], axis=1), N_EXPERTS - 1).astype(I32)
    n_used = (pad_end[-1:] // EXPERT_BLOCK).astype(I32)

    dest_of = lambda topi, rank: (jnp.take(pad_start, topi[:, :TOP_K]) + rank[:, :TOP_K]).reshape(-1)
    dest_p, dest_s = dest_of(topi_p, rank_p), dest_of(topi_s, rank_s)
    xs = _dispatch(pad_start + counts, padded - counts, n_used, x1r_p, x1r_s,
                   jnp.concatenate([dest_p, dest_s]), n_rows)

    ys = _experts(block_expert, n_used, xs, w_gu.astype(F32), b_gu.astype(F32)[:, None, :],
                  w_dn.astype(F32), b_dn.astype(F32)[:, None, :])

    l2g, l2b = vec(ln2_g), vec(ln2_b)
    y_p = _combine(resid_p, gate_p, dest_p, ys, l2g, l2b)
    y_s = _combine(resid_s, gate_s, dest_s, ys, l2g, l2b)

    shp = lambda a, b_, n: a.reshape(b_, n, N_HEADS, HEAD_DIM)
    rows_first = lambda a: jnp.transpose(a, (0, 3, 1, 2))
    return (y_p.reshape(nb, seq, d), y_s.reshape(sb, new, d),
            rows_first(k_p), rows_first(v_p), lf_p.reshape(nb, seq, N_HEADS),
            shp(k_s, sb, new), shp(v_s, sb, new), lf_s[:, :N_HEADS].reshape(sb, new, N_HEADS),
            va_s.reshape(sb, new, N_GROUPS, d // N_GROUPS))


def kernel(x_prompt, x_sample, cache_fox_k, cache_fox_v, cache_fox_logf, p_prompt, p_sample, w_in, b_in, lnv_g, lnv_b, w_s, b_s, w_pa, w_pb, w_o, ln1_g, ln1_b, w_router, b_router, w_gu, b_gu, w_dn, b_dn, w_ple_gate, b_ple_gate, w_ple_proj, ln2_g, ln2_b):
    depth = w_in.shape[0]
    assert depth == 1, "the layer loop below carries one layer"
    dn_alpha = float((2 * depth) ** 0.25)
    outs = _layer(x_prompt, x_sample, cache_fox_k[0], cache_fox_v[0], cache_fox_logf[0], p_prompt[0], p_sample[0],
                  w_in[0], b_in[0], lnv_g[0], lnv_b[0], w_s[0], b_s[0], w_pa[0], w_pb[0], w_o[0], ln1_g[0], ln1_b[0],
                  w_router[0], b_router[0], w_gu[0], b_gu[0], w_dn[0], b_dn[0], w_ple_gate[0], b_ple_gate[0],
                  w_ple_proj[0], ln2_g[0], ln2_b[0], dn_alpha)
    y_p, y_s, k_p, v_p, lf_p, k_s, v_s, lf_s, va_s = outs
    lead = lambda a: a[None]
    return (y_p, y_s, lead(k_p), lead(v_p), lead(lf_p), lead(k_s), lead(v_s), lead(lf_s), lead(va_s))
```

```python
import functools

import jax
import jax.numpy as jnp
import numpy as np
from jax import lax
from jax.experimental import pallas as pl
from jax.experimental.pallas import tpu as pltpu

F32 = jnp.float32
BF16 = jnp.bfloat16
I32 = jnp.int32

LANES = 128
SUBLANES = 8
D_MODEL = 1024
N_HEADS = 16
HEAD_DIM = 64
N_GROUPS = 8
N_EXPERTS = 32
TOP_K = 4
SWIGLU_ALPHA = 1.702
SWIGLU_LIMIT = 7.0
LN_EPS = 1e-5
NEG_BIG = -1e30
LOG2_E = 1.4426950408889634
EXP_ROWS = 32
DEN_ROWS = 16

TOKEN_TILE = 256
Q_TILE = 512
ATTN_HEADS = 2
SAMPLE_CHUNK = 2048
EXPERT_BLOCK = 512
DISPATCH_TILE = 1024
COMBINE_TILE = 256
DMA_UNROLL = 8
VMEM_LIMIT = 56 * 1024 * 1024

F_MID_SHIFT = N_HEADS
F_LO_SHIFT = 2 * N_HEADS
F_ONE_LANE = 3 * N_HEADS


def _dot(a, b):
    return jnp.dot(a, b, preferred_element_type=F32)


def _dot_nt(a, b):
    return lax.dot_general(a, b, (((1,), (1,)), ((), ())), preferred_element_type=F32)


def _layer_norm(x, g, b):
    mu = jnp.mean(x, axis=-1, keepdims=True)
    xc = x - mu
    var = jnp.mean(xc * xc, axis=-1, keepdims=True)
    return xc * lax.rsqrt(var + LN_EPS) * g + b


def _log_sigmoid(z):
    return jnp.minimum(z, 0.0) - jnp.log1p(jnp.exp(-jnp.abs(z)))


def _split3(x):
    hi = x.astype(BF16).astype(F32)
    r = x - hi
    mid = r.astype(BF16).astype(F32)
    lo = (r - mid).astype(BF16).astype(F32)
    return hi, mid, lo


def _tri(n, strict):
    r = lax.broadcasted_iota(I32, (n, n), 0)
    c = lax.broadcasted_iota(I32, (n, n), 1)
    keep = (r > c) if strict else (r >= c)
    return jnp.where(keep, 1.0, 0.0).astype(F32)


def _cumsum_rows(x):
    n = x.shape[0]
    hi, mid, lo = _split3(x)
    cat = jnp.concatenate([hi, mid, lo], axis=1).astype(BF16)
    c = _dot(_tri(n, strict=False).astype(BF16), cat)
    return (c[:, 2 * LANES:] + c[:, LANES:2 * LANES]) + c[:, :LANES]


def _proj_sections(x_ref, wuv, wkv, wf, wg, buv, bkv, bf_, bg, lng, lnb, u_o, g_o):
    xb = x_ref[...].astype(BF16)
    pu = _dot(xb, wuv[:, :D_MODEL]) + buv[:, :D_MODEL]
    u_o[...] = jax.nn.gelu(pu).astype(BF16)
    pv = _dot(xb, wuv[:, D_MODEL:]) + buv[:, D_MODEL:]
    va = _layer_norm(jax.nn.gelu(pv), lng[...], lnb[...])
    k = _dot(xb, wkv[:, :D_MODEL]) + bkv[:, :D_MODEL]
    v = _dot(xb, wkv[:, D_MODEL:]) + bkv[:, D_MODEL:]
    g_o[...] = jax.nn.sigmoid(_dot(xb, wg[...]) + bg[...]).astype(BF16)
    f = _dot(xb, wf[...]) + bf_[...]
    lane = lax.broadcasted_iota(I32, f.shape, 1)
    lf = jnp.where(lane < N_HEADS, _log_sigmoid(f), 0.0)
    return xb, va, k, v, lf


def _proj_prompt_kernel(x_ref, wuv, wq, wkv, wf, wg, buv, bq, bkv, bf_, bg, lng, lnb, eq, ek,
                        u_o, va_o, k_o, v_o, lf_o, g_o, qa_o, ka_o, vt_o, carry, *, tiles_per_seq):
    i = pl.program_id(0)
    xb, va, k, v, lf = _proj_sections(x_ref, wuv, wkv, wf, wg, buv, bkv, bf_, bg, lng, lnb, u_o, g_o)
    va_o[...] = va.astype(BF16)
    v_t = v.T
    k_o[0] = k.T.reshape(k_o.shape[1:])
    v_o[0] = v_t.reshape(v_o.shape[1:])
    vt_o[0, 0] = v_t.astype(BF16)
    lf_o[...] = lf[:, :N_HEADS]

    @pl.when(i % tiles_per_seq == 0)
    def _():
        carry[...] = jnp.zeros_like(carry)

    tm = lf.shape[0]
    fcum = carry[...] + _cumsum_rows(lf)
    carry[...] = fcum[tm - 1:tm, :]

    hi, mid, lo = _split3(fcum * LOG2_E)
    lane = lax.broadcasted_iota(I32, fcum.shape, 1)
    pack = (hi + pltpu.roll(mid, F_MID_SHIFT, 1) + pltpu.roll(lo, F_LO_SHIFT, 1)
            + jnp.where(lane == F_ONE_LANE, 1.0, 0.0)).astype(BF16)
    add_q = _dot(pack, eq[...])
    add_k = _dot(pack, ek[...])
    q = (_dot(xb, wq[...]) + bq[...]) * (HEAD_DIM ** -0.5 * LOG2_E)
    low =lane < HEAD_DIM
    for g in range(N_GROUPS):
        sl = slice(g * LANES, (g + 1) * LANES)
        ev = slice(2 * g * LANES, (2 * g + 1) * LANES)
        od = slice((2 * g + 1) * LANES, (2 * g + 2) * LANES)
        qa_o[:, ev] = jnp.where(low, q[:, sl], add_q[:, ev]).astype(BF16)
        qa_o[:, od] = jnp.where(low, add_q[:, od], q[:, sl]).astype(BF16)
        ka_o[:, ev] = jnp.where(low, k[:, sl], add_k[:, ev]).astype(BF16)
        ka_o[:, od] = jnp.where(low, add_k[:, od], k[:, sl]).astype(BF16)


def _proj_sample_kernel(x_ref, wuv, wq, wkv, wf, wg, buv, bq, bkv, bf_, bg, lng, lnb,
                        u_o, va_o, k_o, v_o, lf_o, g_o, q_o):
    xb, va, k, v, lf = _proj_sections(x_ref, wuv, wkv, wf, wg, buv, bkv, bf_, bg, lng, lnb, u_o, g_o)
    k_o[...] = k
    v_o[...] = v
    va_o[...] = va
    lf_o[...] = lf
    q_o[...] = (_dot(xb, wq[...]) + bq[...]) * (HEAD_DIM ** -0.5)


def _const_spec(shape):
    nd = len(shape)
    return pl.BlockSpec(shape, lambda *_: (0,) * nd, pipeline_mode=pl.Buffered(1))


def _proj_weights(w_in, b_in):
    d = D_MODEL
    off_q, off_k, off_f = 2 * d, 3 * d, 5 * d
    off_ga = off_f + N_HEADS
    wuv = w_in[:, :off_q].astype(BF16)
    wq = w_in[:, off_q:off_k].astype(BF16)
    wkv = w_in[:, off_k:off_f].astype(BF16)
    wf = jnp.pad(w_in[:, off_f:off_ga], ((0, 0), (0, LANES - N_HEADS))).astype(BF16)
    wg = w_in[:, off_ga:].astype(BF16)
    b = b_in.reshape(1, -1).astype(F32)
    buv, bq, bkv = b[:, :off_q], b[:, off_q:off_k], b[:, off_k:off_f]
    bf_ = jnp.pad(b[:, off_f:off_ga], ((0, 0), (0, LANES - N_HEADS)))
    bg = b[:, off_ga:]
    return (wuv, wq, wkv, wf, wg), (buv, bq, bkv, bf_, bg)


def _spread_matrices():
    eq_np = np.zeros((LANES, 2 * D_MODEL), np.float32)
    ek_np = np.zeros((LANES, 2 * D_MODEL), np.float32)
    for h in range(N_HEADS):
        base = LANES * h + (HEAD_DIM if h % 2 == 0 else 0)
        for c, shift in enumerate((0, F_MID_SHIFT, F_LO_SHIFT)):
            eq_np[shift + h, base + c] = 1.0
            eq_np[F_ONE_LANE, base + 3 + c] = 1.0
            ek_np[F_ONE_LANE, base + c] = 1.0
            ek_np[shift + h, base + 3 + c] = -1.0
    return jnp.asarray(eq_np, BF16), jnp.asarray(ek_np, BF16)


def _project_prompt(x2d, seq, ws, bs, lng, lnb):
    t = x2d.shape[0]
    tm = min(TOKEN_TILE, seq)
    nt = t // tm
    nb = t // seq
    eq, ek = _spread_matrices()
    consts = list(ws) + list(bs) + [lng, lnb, eq, ek]
    row = lambda w: pl.BlockSpec((tm, w), lambda i: (i, 0))
    out_shape = (
        jax.ShapeDtypeStruct((t, D_MODEL), BF16),
        jax.ShapeDtypeStruct((t, D_MODEL), BF16),
        jax.ShapeDtypeStruct((nb, N_HEADS, HEAD_DIM, seq), F32),
        jax.ShapeDtypeStruct((nb, N_HEADS, HEAD_DIM, seq), F32),
        jax.ShapeDtypeStruct((t, N_HEADS), F32),
        jax.ShapeDtypeStruct((t, 2 * D_MODEL), BF16),
        jax.ShapeDtypeStruct((t, 2 * D_MODEL), BF16),
        jax.ShapeDtypeStruct((t, 2 * D_MODEL), BF16),
        jax.ShapeDtypeStruct((nb, seq // tm, D_MODEL, tm), BF16),
    )
    tiles_per_seq = seq // tm
    heads = pl.BlockSpec((1, N_HEADS, HEAD_DIM, tm), lambda i: (i // tiles_per_seq, 0, 0, i % tiles_per_seq))
    out_specs = (row(D_MODEL), row(D_MODEL), heads, heads, row(N_HEADS),
                 row(2 * D_MODEL), row(2 * D_MODEL), row(2 * D_MODEL),
                 pl.BlockSpec((1, 1, D_MODEL, tm), lambda i: (i // tiles_per_seq, i % tiles_per_seq, 0, 0)))
    return pl.pallas_call(
        functools.partial(_proj_prompt_kernel, tiles_per_seq=tiles_per_seq),
        grid=(nt,),
        in_specs=[row(D_MODEL)] + [_const_spec(c.shape) for c in consts],
        out_specs=out_specs,
        out_shape=out_shape,
        scratch_shapes=[pltpu.VMEM((1, LANES), F32)],
        compiler_params=pltpu.CompilerParams(dimension_semantics=("arbitrary",), vmem_limit_bytes=VMEM_LIMIT),
        name="proj_prompt",
    )(x2d, *consts)


def _project_sample(x2d, ws, bs, lng, lnb):
    t = x2d.shape[0]
    tm = min(TOKEN_TILE, t)
    nt = t // tm
    consts = list(ws) + list(bs) + [lng, lnb]
    row = lambda w: pl.BlockSpec((tm, w), lambda i: (i, 0))
    out_shape = (
        jax.ShapeDtypeStruct((t, D_MODEL), BF16),
        jax.ShapeDtypeStruct((t, D_MODEL), F32),
        jax.ShapeDtypeStruct((t, D_MODEL), F32),
        jax.ShapeDtypeStruct((t, D_MODEL), F32),
        jax.ShapeDtypeStruct((t, LANES), F32),
        jax.ShapeDtypeStruct((t, 2 * D_MODEL), BF16),
        jax.ShapeDtypeStruct((t, D_MODEL), F32),
    )
    out_specs = (row(D_MODEL), row(D_MODEL), row(D_MODEL), row(D_MODEL), row(LANES), row(2 * D_MODEL), row(D_MODEL))
    return pl.pallas_call(
        _proj_sample_kernel,
        grid=(nt,),
        in_specs=[row(D_MODEL)] + [_const_spec(c.shape) for c in consts],
        out_specs=out_specs,
        out_shape=out_shape,
        compiler_params=pltpu.CompilerParams(dimension_semantics=("arbitrary",), vmem_limit_bytes=VMEM_LIMIT),
        name="proj_sample",
    )(x2d, *consts)


def _attn_prompt_kernel(q_ref, k_ref, vt_ref, o_ref, m_ref, acc_ref, ot_ref,
                        s0, s1, p0, p1, a0, a1, c0, c1, qt_sc, *, tq, tk):
    qi = pl.program_id(2)
    nh = m_ref.shape[0]
    assert tq == 2 * tk, "the pipeline below is written for two diagonal spans per query tile"
    m_ref[...] = jnp.full(m_ref.shape, NEG_BIG, F32)
    acc_ref[...] = jnp.zeros(acc_ref.shape, F32)
    p1[...] = jnp.zeros(p1.shape, BF16)
    a1[...] = jnp.ones(a1.shape, F32)
    tv = vt_ref.shape[3]
    ones = jnp.ones((DEN_ROWS, tv), BF16)
    lanes = lambda hh: slice(hh * LANES, (hh + 1) * LANES)

    def score(t, s_slot, c_slot):
        start = pl.multiple_of(t * tk, tk)
        for hh in range(nh):
            st = _dot(k_ref[pl.ds(start, tk), lanes(hh)], qt_sc[hh])
            s_slot[hh] = st
            c_slot[hh] = jnp.max(st, axis=0, keepdims=True)

    def softmax(s_slot, c_slot, p_slot, a_slot, mask_offset):
        for hh in range(nh):
            if mask_offset is None:
                col_max = c_slot[hh]
            else:
                key_row = lax.broadcasted_iota(I32, (tk, tq), 0)
                qry_col = lax.broadcasted_iota(I32, (tk, tq), 1)
                s_slot[hh] = jnp.where(key_row + mask_offset <= qry_col, s_slot[hh], NEG_BIG)
                col_max = jnp.max(s_slot[hh], axis=0, keepdims=True)
            m_old = m_ref[hh]
            m_new = jnp.maximum(m_old, col_max)
            a_slot[hh] = jnp.exp2(m_old - m_new)
            for r in range(0, tk, EXP_ROWS):
                rows = slice(r, r + EXP_ROWS)
                p_slot[hh, rows, :] = jnp.exp2((s_slot[hh, rows, :] - m_new).astype(BF16))
            m_ref[hh] = m_new

    def value(t, p_slot, a_slot):
        span = jnp.maximum(t, 0)
        for hh in range(nh):
            pv = None
            for d in range(tk // tv):
                vt = vt_ref[0, span * (tk // tv) + d, hh * HEAD_DIM:(hh + 1) * HEAD_DIM, :]
                part = _dot(jnp.concatenate([vt, ones], axis=0), p_slot[hh, d * tv:(d + 1) * tv, :])
                pv = part if pv is None else pv + part
            acc_ref[hh] = a_slot[hh] * acc_ref[hh] + pv

    for hh in range(nh):
        qt_sc[hh] = q_ref[:, lanes(hh)].astype(F32).T.astype(BF16)
    score(0, s0, c0)

    def two_steps(ii):
        t = 2 * ii + 1
        value(t - 2, p1, a1)
        score(t, s1, c1)
        softmax(s0, c0, p0, a0, None)
        value(t - 1, p0, a0)
        score(t + 1, s0, c0)
        softmax(s1, c1, p1, a1, None)

    def body(jj, c):
        two_steps(2 * jj)
        two_steps(2 * jj + 1)
        return c

    lax.fori_loop(0, qi // 2, body, 0)

    @pl.when(qi % 2 == 1)
    def _():
        two_steps(qi - 1)

    t = 2 * qi + 1
    value(t - 2, p1, a1)
    score(t, s1, c1)
    softmax(s0, c0, p0, a0, 0)
    value(t - 1, p0, a0)
    softmax(s1, c1, p1, a1, tk)
    value(t, p1, a1)
    for hh in range(nh):
        ot_ref[hh * HEAD_DIM:(hh + 1) * HEAD_DIM, :] = acc_ref[hh, :HEAD_DIM] / acc_ref[hh, HEAD_DIM:HEAD_DIM + 1]
    o_ref[...] = ot_ref[...].T.astype(BF16)


def _attend_prompt(q_aug, k_aug, v_t, nb, seq):
    t = q_aug.shape[0]
    tv = v_t.shape[3]
    tq = min(Q_TILE, seq)
    tk = tq // 2
    nq = seq // tq
    nh = ATTN_HEADS
    return pl.pallas_call(
        functools.partial(_attn_prompt_kernel, tq=tq, tk=tk),
        grid=(nb, N_HEADS // nh, nq),
        in_specs=[
            pl.BlockSpec((tq, nh * LANES), lambda b, g, i: (b * nq + i, g)),
            pl.BlockSpec((seq, nh * LANES), lambda b, g, i: (b, g)),
            pl.BlockSpec((1, seq // tv, nh * HEAD_DIM, tv), lambda b, g, i: (b, 0, g, 0)),
        ],
        out_specs=pl.BlockSpec((tq, nh * HEAD_DIM), lambda b, g, i: (b * nq + i, g)),
        out_shape=jax.ShapeDtypeStruct((t, D_MODEL), BF16),
        scratch_shapes=[pltpu.VMEM((nh, 1, tq), F32), pltpu.VMEM((nh, HEAD_DIM + DEN_ROWS, tq), F32),
                        pltpu.VMEM((nh * HEAD_DIM, tq), F32),
                        pltpu.VMEM((nh, tk, tq), F32), pltpu.VMEM((nh, tk, tq), F32),
                        pltpu.VMEM((nh, tk, tq), BF16), pltpu.VMEM((nh, tk, tq), BF16),
                        pltpu.VMEM((nh, 1, tq), F32), pltpu.VMEM((nh, 1, tq), F32),
                        pltpu.VMEM((nh, 1, tq), F32), pltpu.VMEM((nh, 1, tq), F32),
                        pltpu.VMEM((nh, LANES, tq), BF16)],
        compiler_params=pltpu.CompilerParams(
            dimension_semantics=("arbitrary", "arbitrary", "arbitrary"), vmem_limit_bytes=VMEM_LIMIT),
        name="attn_prompt",
    )(q_aug, k_aug, v_t)


def _attn_sample_kernel(q_ref, k_ref, v_ref, lf_ref, ck_ref, cv_ref, clf_ref, o_ref,
                        fct_sc, fn_sc, m_sc, l_sc, acc_sc, *, n_chunks):
    c = pl.program_id(1)
    chunk = ck_ref.shape[3]
    new = q_ref.shape[1]

    @pl.when(c == 0)
    def _():
        carry = jnp.zeros((1, LANES), F32)
        for cc in range(n_chunks):
            fc = carry + _cumsum_rows(clf_ref[0, cc * chunk:(cc + 1) * chunk, :])
            carry = fc[chunk - 1:chunk, :]
            fct_sc[cc] = fc.T
        fn_sc[...] = carry + _cumsum_rows(lf_ref[0])
        m_sc[...] = jnp.full(m_sc.shape, NEG_BIG, F32)
        l_sc[...] = jnp.zeros(l_sc.shape, F32)
        acc_sc[...] = jnp.zeros(acc_sc.shape, F32)

    lane = lax.broadcasted_iota(I32, (new, LANES), 1)
    f_new = fn_sc[...]

    def attend(score, weigh, f_keys_t, mask):
        for g in range(N_GROUPS):
            qg = q_ref[0, :, g * LANES:(g + 1) * LANES]
            for hh in range(2):
                h = 2 * g + hh
                mine = (lane < HEAD_DIM) if hh == 0 else (lane >= HEAD_DIM)
                qm = jnp.where(mine, qg, 0.0).astype(BF16)
                s = score(qm, g) + f_new[:, h:h + 1] - f_keys_t[h:h + 1, :]
                if mask is not None:
                    s = jnp.where(mask, s, NEG_BIG)
                m_old = m_sc[h]
                m_new = jnp.maximum(m_old, jnp.max(s, axis=1, keepdims=True))
                alpha = jnp.exp(m_old - m_new)
                p = jnp.exp(s - m_new)
                l_sc[h] = alpha * l_sc[h] + jnp.sum(p, axis=1, keepdims=True)
                acc_sc[h] = alpha * acc_sc[h] + weigh(p.astype(BF16), g)
                m_sc[h] = m_new

    pair_t = lambda ref, g: ref[0, 2 * g:2 * g + 2].reshape(LANES, chunk).astype(BF16)
    attend(lambda qm, g: _dot(qm, pair_t(ck_ref, g)), lambda p, g: _dot_nt(p, pair_t(cv_ref, g)), fct_sc[c], None)

    @pl.when(c == n_chunks - 1)
    def _():
        causal = lax.broadcasted_iota(I32, (new, new), 1) <= lax.broadcasted_iota(I32, (new, new), 0)
        cols = lambda ref, g: ref[0, :, g * LANES:(g + 1) * LANES].astype(BF16)
        attend(lambda qm, g: _dot_nt(qm, cols(k_ref, g)), lambda p, g: _dot(p, cols(v_ref, g)), f_new.T, causal)
        for g in range(N_GROUPS):
            even = acc_sc[2 * g] / l_sc[2 * g]
            odd = acc_sc[2 * g + 1] / l_sc[2 * g + 1]
            o_ref[:, g * LANES:(g + 1) * LANES] = jnp.where(lane < HEAD_DIM, even, odd).astype(BF16)


def _attend_sample(q, k, v, lf, cache_k, cache_v, cache_lf):
    nb, new, _ = q.shape
    past = cache_k.shape[3]
    chunk = min(SAMPLE_CHUNK, past)
    n_chunks = past // chunk
    blk = lambda n, w: pl.BlockSpec((1, n, w), lambda b, c: (b, 0, 0))
    cache = pl.BlockSpec((1, N_HEADS, HEAD_DIM, chunk), lambda b, c: (b, 0, 0, c))
    return pl.pallas_call(
        functools.partial(_attn_sample_kernel, n_chunks=n_chunks),
        grid=(nb, n_chunks),
        in_specs=[blk(new, D_MODEL), blk(new, D_MODEL), blk(new, D_MODEL), blk(new, LANES),
                  cache, cache, blk(past, LANES)],
        out_specs=pl.BlockSpec((new, D_MODEL), lambda b, c: (b, 0)),
        out_shape=jax.ShapeDtypeStruct((nb * new, D_MODEL), BF16),
        scratch_shapes=[pltpu.VMEM((n_chunks, LANES, chunk), F32), pltpu.VMEM((new, LANES), F32),
                        pltpu.VMEM((N_HEADS, new, 1), F32), pltpu.VMEM((N_HEADS, new, 1), F32),
                        pltpu.VMEM((N_HEADS, new, LANES), F32)],
        compiler_params=pltpu.CompilerParams(dimension_semantics=("arbitrary", "arbitrary"),
                                             vmem_limit_bytes=VMEM_LIMIT),
        name="attn_sample",
    )(q, k, v, lf, cache_k, cache_v, cache_lf)


def _post_kernel(x_ref, u_ref, va_ref, g_ref, at_ref, p_ref, base_ref, ws_ref, bst_ref, wpa, wpb, wo,
                 l1g, l1b, wr_hi, wr_lo, br, wpg, bpg, wpp,
                 x1_o, resid_o, topi_o, gate_o, rank_o, cnt_o, a_sc, base_sc, *, chunk, dn_alpha):
    i = pl.program_id(0)
    tm = x_ref.shape[0]

    @pl.when(i == 0)
    def _():
        base_sc[...] = base_ref[...]

    tri = _tri(chunk, strict=False)
    for g in range(N_GROUPS):
        sl = slice(g * LANES, (g + 1) * LANES)
        wm = (ws_ref[g, :chunk, :chunk] * tri).astype(BF16)
        bias = bst_ref[:chunk, g:g + 1]
        for c in range(tm // chunk):
            rows = slice(c * chunk, (c + 1) * chunk)
            sg = _dot(wm, va_ref[rows, sl]) + bias
            a_sc[rows, sl] = (u_ref[rows, sl].astype(F32) * sg).astype(BF16)

    pa = _dot(a_sc[...], wpa[...])
    pb = _dot(at_ref[...], wpb[...])
    merged = g_ref[:, :D_MODEL].astype(F32) * pa + g_ref[:, D_MODEL:].astype(F32) * pb
    x1 = _layer_norm(dn_alpha * x_ref[...] + _dot(merged.astype(BF16), wo[...]), l1g[...], l1b[...])
    x1b = x1.astype(BF16)
    x1_o[...] = x1

    ple = jax.nn.sigmoid(_dot(x1b, wpg[...]) + bpg[...]) * _dot(p_ref[...].astype(BF16), wpp[...])
    resid_o[...] = dn_alpha * x1 + ple

    x1_lo = (x1 - x1b.astype(F32)).astype(BF16)
    logits = (_dot(x1b, wr_hi[...]) + (_dot(x1b, wr_lo[...]) + _dot(x1_lo, wr_hi[...]))) + br[...]
    lane = lax.broadcasted_iota(I32, (tm, LANES), 1)
    work = logits
    vals, idxs = [], []
    for _ in range(TOP_K):
        mk = jnp.max(work, axis=1, keepdims=True)
        ik = jnp.min(jnp.where(work == mk, lane, LANES), axis=1, keepdims=True)
        vals.append(mk)
        idxs.append(ik)
        work = jnp.where(lane == ik, -jnp.inf, work)
    exps = [jnp.exp(v - vals[0]) for v in vals]
    den = exps[0] + exps[1] + exps[2] + exps[3]

    lower = _tri(tm, strict=True).astype(BF16)
    run = base_sc[...]
    topi = jnp.zeros((tm, LANES), I32)
    gate = jnp.zeros((tm, LANES), F32)
    rank = jnp.zeros((tm, LANES), F32)
    for k in range(TOP_K):
        onehot = jnp.where(lane == idxs[k], 1.0, 0.0)
        before = _dot(lower, onehot.astype(BF16))
        rk = jnp.sum(onehot * (before + run), axis=1, keepdims=True)
        topi = jnp.where(lane == k, idxs[k], topi)
        gate = jnp.where(lane == k, exps[k] / den, gate)
        rank = jnp.where(lane == k, rk, rank)
        run = run + jnp.sum(onehot, axis=0, keepdims=True)
    base_sc[...] = run
    topi_o[...] = topi
    gate_o[...] = gate
    rank_o[...] = rank.astype(I32)
    cnt_o[...] = run


def _post_attention(x2d, u, va, gates, attn, p2d, base, consts, chunk, dn_alpha):
    t = x2d.shape[0]
    tm = min(TOKEN_TILE, t)
    nt = t // tm
    row = lambda w: pl.BlockSpec((tm, w), lambda i: (i, 0))
    ple_dim = p2d.shape[1]
    out_shape = (
        jax.ShapeDtypeStruct((t, D_MODEL), F32),
        jax.ShapeDtypeStruct((t, D_MODEL), F32),
        jax.ShapeDtypeStruct((t, LANES), I32),
        jax.ShapeDtypeStruct((t, LANES), F32),
        jax.ShapeDtypeStruct((t, LANES), I32),
        jax.ShapeDtypeStruct((1, LANES), F32),
    )
    out_specs = (row(D_MODEL), row(D_MODEL), row(LANES), row(LANES), row(LANES),
                 pl.BlockSpec((1, LANES), lambda i: (0, 0)))
    return pl.pallas_call(
        functools.partial(_post_kernel, chunk=chunk, dn_alpha=dn_alpha),
        grid=(nt,),
        in_specs=[row(D_MODEL), row(D_MODEL), row(D_MODEL), row(2 * D_MODEL), row(D_MODEL), row(ple_dim),
                  _const_spec(base.shape)] + [_const_spec(c.shape) for c in consts],
        out_specs=out_specs,
        out_shape=out_shape,
        scratch_shapes=[pltpu.VMEM((tm, D_MODEL), BF16), pltpu.VMEM((1, LANES), F32)],
        compiler_params=pltpu.CompilerParams(dimension_semantics=("arbitrary",), vmem_limit_bytes=VMEM_LIMIT),
        name="post_attention",
    )(x2d, u, va, gates, attn, p2d, base, *consts)


def _dispatch_kernel(fill_start_ref, fill_len_ref, nu_ref, xp_ref, xn_ref, dest_ref, xs_out, zero_sc, sem, zsem,
                     *, prompt_steps, n_blocks):
    i = pl.program_id(0)
    td = xp_ref.shape[0]

    @pl.when(i == 0)
    def _():
        zero_sc[...] = jnp.zeros(zero_sc.shape, zero_sc.dtype)

        def zero_fill(wait):
            def go(copy):
                copy.wait() if wait else copy.start()

            def per_expert(e, c):
                n = fill_len_ref[e]
                base = fill_start_ref[e]
                head = jnp.bitwise_and(n, SUBLANES - 1)
                for r in range(SUBLANES - 1):
                    @pl.when(r < head)
                    def _(r=r):
                        go(pltpu.make_async_copy(zero_sc.at[pl.ds(0, 1)], xs_out.at[pl.ds(base + r, 1)], zsem))

                piece = EXPERT_BLOCK // 2
                while piece >= SUBLANES:
                    offset = pl.multiple_of(base + head + jnp.bitwise_and(n, -(2 * piece)), SUBLANES)

                    @pl.when(jnp.bitwise_and(n, piece) != 0)
                    def _(piece=piece, offset=offset):
                        go(pltpu.make_async_copy(zero_sc.at[pl.ds(0, piece)], xs_out.at[pl.ds(offset, piece)], zsem))

                    piece //= 2
                return c

            lax.fori_loop(0, N_EXPERTS, per_expert, 0)

            def per_block(j, c):
                row0 = pl.multiple_of(j * EXPERT_BLOCK, EXPERT_BLOCK)
                go(pltpu.make_async_copy(zero_sc, xs_out.at[pl.ds(row0, EXPERT_BLOCK)], zsem))
                return c

            lax.fori_loop(nu_ref[0], n_blocks, per_block, 0)

        zero_fill(wait=False)
        zero_fill(wait=True)

    def scatter(x_ref):
        def body(j, c):
            for u in range(DMA_UNROLL):
                tok = j * (DMA_UNROLL // TOP_K) + u // TOP_K
                pltpu.make_async_copy(x_ref.at[pl.ds(tok, 1)], xs_out.at[pl.ds(dest_ref[j * DMA_UNROLL + u], 1)],
                                      sem).start(priority=u % 2)
            return c

        lax.fori_loop(0, td * TOP_K // DMA_UNROLL, body, 0)
        for _ in range(TOP_K):
            pltpu.make_async_copy(x_ref, xs_out.at[pl.ds(0, td)], sem).wait()

    @pl.when(i < prompt_steps)
    def _():
        scatter(xp_ref)

    @pl.when(i >= prompt_steps)
    def _():
        scatter(xn_ref)


def _dispatch(fill_start, fill_len, n_used, x1_prompt, x1_sample, dest, n_rows):
    tp, w = x1_prompt.shape
    ts = x1_sample.shape[0]
    td = min(DISPATCH_TILE, ts)
    prompt_steps, sample_steps = tp // td, ts // td
    grid_spec = pltpu.PrefetchScalarGridSpec(
        num_scalar_prefetch=3,
        grid=(prompt_steps + sample_steps,),
        in_specs=[pl.BlockSpec((td, w), lambda i, *_: (jnp.minimum(i, prompt_steps - 1), 0)),
                  pl.BlockSpec((td, w), lambda i, *_: (jnp.maximum(i - prompt_steps, 0), 0)),
                  pl.BlockSpec((td * TOP_K,), lambda i, *_: (i,), memory_space=pltpu.SMEM)],
        out_specs=pl.BlockSpec(memory_space=pl.ANY),
        scratch_shapes=[pltpu.VMEM((EXPERT_BLOCK, w), x1_prompt.dtype),
                        pltpu.SemaphoreType.DMA(()), pltpu.SemaphoreType.DMA(())],
    )
    return pl.pallas_call(
        functools.partial(_dispatch_kernel, prompt_steps=prompt_steps, n_blocks=n_rows // EXPERT_BLOCK),
        grid_spec=grid_spec,
        out_shape=jax.ShapeDtypeStruct((n_rows, w), x1_prompt.dtype),
        compiler_params=pltpu.CompilerParams(dimension_semantics=("arbitrary",), vmem_limit_bytes=VMEM_LIMIT),
        name="dispatch",
    )(fill_start, fill_len, n_used, x1_prompt, x1_sample, dest)


def _expert_kernel(be_ref, nu_ref, xs_ref, wgu, bgu, wdn, bdn, ys_o, wgu_bf, wdn_bf):
    j = pl.program_id(0)
    used = j < nu_ref[0]
    new_expert = jnp.logical_or(j == 0, be_ref[j] != be_ref[jnp.maximum(j - 1, 0)])

    @pl.when(jnp.logical_not(used))
    def _():
        ys_o[...] = jnp.zeros(ys_o.shape, ys_o.dtype)

    @pl.when(jnp.logical_and(used, new_expert))
    def _():
        wgu_bf[...] = wgu[0].astype(BF16)
        wdn_bf[...] = wdn[0].astype(BF16)

    @pl.when(used)
    def _():
        h = _dot(xs_ref[...].astype(BF16), wgu_bf[...]) + bgu[0]
        d_e = h.shape[1] // 2
        gate = jnp.minimum(h[:, :d_e], SWIGLU_LIMIT)
        up = jnp.clip(h[:, d_e:], -SWIGLU_LIMIT, SWIGLU_LIMIT)
        glu = gate * jax.nn.sigmoid(gate * SWIGLU_ALPHA)
        ys_o[...] = _dot(((up + 1.0) * glu).astype(BF16), wdn_bf[...]) + bdn[0]


def _experts(block_expert, n_used, xs, wgu, bgu, wdn, bdn):
    rows, w = xs.shape
    nblk = rows // EXPERT_BLOCK
    d_in, d_gu = wgu.shape[1], wgu.shape[2]
    d_e, d_out = wdn.shape[1], wdn.shape[2]
    grid_spec = pltpu.PrefetchScalarGridSpec(
        num_scalar_prefetch=2,
        grid=(nblk,),
        in_specs=[
            pl.BlockSpec((EXPERT_BLOCK, w), lambda j, be, nu: (j, 0)),
            pl.BlockSpec((1, d_in, d_gu), lambda j, be, nu: (be[j], 0, 0)),
            pl.BlockSpec((1, 1, d_gu), lambda j, be, nu: (be[j], 0, 0)),
            pl.BlockSpec((1, d_e, d_out), lambda j, be, nu: (be[j], 0, 0)),
            pl.BlockSpec((1, 1, d_out), lambda j, be, nu: (be[j], 0, 0)),
        ],
        out_specs=pl.BlockSpec((EXPERT_BLOCK, d_out), lambda j, be, nu: (j, 0)),
        scratch_shapes=[pltpu.VMEM((d_in, d_gu), BF16), pltpu.VMEM((d_e, d_out), BF16)],
    )
    return pl.pallas_call(
        _expert_kernel,
        grid_spec=grid_spec,
        out_shape=jax.ShapeDtypeStruct((rows, d_out), F32),
        compiler_params=pltpu.CompilerParams(dimension_semantics=("arbitrary",), vmem_limit_bytes=VMEM_LIMIT),
        name="experts",
    )(block_expert, n_used, xs, wgu, bgu, wdn, bdn)


def _final_kernel(resid_ref, gate_ref, dest_ref, dest_next_ref, ys_ref, lng, lnb, y_o, buf, sem, *, n):
    i = pl.program_id(0)
    tg = resid_ref.shape[0]

    def gather(idx_ref, slot):
        def body(j, c):
            for u in range(DMA_UNROLL):
                tok = j * (DMA_UNROLL // TOP_K) + u // TOP_K
                pltpu.make_async_copy(ys_ref.at[pl.ds(idx_ref[j * DMA_UNROLL + u], 1)],
                                      buf.at[slot, u % TOP_K, pl.ds(tok, 1)], sem.at[slot]).start(priority=u % 2)
            return c

        lax.fori_loop(0, tg * TOP_K // DMA_UNROLL, body, 0)

    @pl.when(i == 0)
    def _():
        gather(dest_ref, 0)

    @pl.when(i + 1 < n)
    def _():
        gather(dest_next_ref, (i + 1) % 2)

    slot = i % 2
    for k in range(TOP_K):
        pltpu.make_async_copy(ys_ref.at[pl.ds(0, tg)], buf.at[slot, k], sem.at[slot]).wait()
    acc = resid_ref[...]
    for k in range(TOP_K):
        acc = acc + gate_ref[:, k:k + 1] * buf[slot, k]
    y_o[...] = _layer_norm(acc, lng[...], lnb[...])


def _combine(resid, gate, dest, ys, lng, lnb):
    t = resid.shape[0]
    tg = min(COMBINE_TILE, t)
    nt = t // tg
    row = lambda w: pl.BlockSpec((tg, w), lambda i: (i, 0))
    return pl.pallas_call(
        functools.partial(_final_kernel, n=nt),
        grid=(nt,),
        in_specs=[row(D_MODEL), row(LANES),
                  pl.BlockSpec((tg * TOP_K,), lambda i: (i,), memory_space=pltpu.SMEM),
                  pl.BlockSpec((tg * TOP_K,), lambda i: (jnp.minimum(i + 1, nt - 1),), memory_space=pltpu.SMEM),
                  pl.BlockSpec(memory_space=pl.ANY),
                  _const_spec(lng.shape), _const_spec(lnb.shape)],
        out_specs=row(D_MODEL),
        out_shape=jax.ShapeDtypeStruct((t, D_MODEL), F32),
        scratch_shapes=[pltpu.VMEM((2, TOP_K, tg, ys.shape[1]), F32), pltpu.SemaphoreType.DMA((2,))],
        compiler_params=pltpu.CompilerParams(dimension_semantics=("arbitrary",), vmem_limit_bytes=VMEM_LIMIT),
        name="combine",
    )(resid, gate, dest, dest, ys, lng, lnb)


def _layer(x_prompt, x_sample, cache_k, cache_v, cache_logf, p_prompt, p_sample,
           w_in, b_in, lnv_g, lnv_b, w_s, b_s, w_pa, w_pb, w_o, ln1_g, ln1_b,
           w_router, b_router, w_gu, b_gu, w_dn, b_dn, w_ple_gate, b_ple_gate, w_ple_proj, ln2_g, ln2_b,
           dn_alpha):
    nb, seq, d = x_prompt.shape
    sb, new, _ = x_sample.shape
    past = cache_k.shape[1]
    tp, ts = nb * seq, sb * new
    vec = lambda a: a.reshape(1, -1).astype(F32)

    ws, bs = _proj_weights(w_in, b_in)
    lng, lnb = vec(lnv_g), vec(lnv_b)
    xp = x_prompt.reshape(tp, d)
    xsm = x_sample.reshape(ts, d)
    u_p, va_p, k_p, v_p, lf_p, g_p, qa_p, ka_p, vt_p = _project_prompt(xp, seq, ws, bs, lng, lnb)
    u_s, va_s, k_s, v_s, lf_s, g_s, q_s = _project_sample(xsm, ws, bs, lng, lnb)

    attn_p = _attend_prompt(qa_p, ka_p, vt_p, nb, seq)
    cache_lf = jnp.pad(cache_logf.astype(F32), ((0, 0), (0, 0), (0, LANES - N_HEADS)))
    rows_last = lambda a: jnp.transpose(a, (0, 2, 3, 1))
    attn_s = _attend_sample(q_s.reshape(sb, new, d), k_s.reshape(sb, new, d), v_s.reshape(sb, new, d),
                            lf_s.reshape(sb, new, LANES), rows_last(cache_k), rows_last(cache_v), cache_lf)

    wr = jnp.pad(w_router.astype(F32), ((0, 0), (0, LANES - N_EXPERTS)))
    wr_hi = wr.astype(BF16)
    wr_lo = (wr - wr_hi.astype(F32)).astype(BF16)
    br = jnp.concatenate([b_router.astype(F32), jnp.full((LANES - N_EXPERTS,), NEG_BIG, F32)]).reshape(1, LANES)
    spatial = b_s.shape[1]
    bst = jnp.pad(b_s.astype(F32).T, ((0, 0), (0, LANES - b_s.shape[0])))
    post_consts = [w_s.astype(F32), bst, w_pa.astype(BF16), w_pb.astype(BF16), w_o.astype(BF16),
                   vec(ln1_g), vec(ln1_b), wr_hi, wr_lo, br,
                   w_ple_gate.astype(BF16), vec(b_ple_gate), w_ple_proj.astype(BF16)]
    base0 = jnp.zeros((1, LANES), F32)
    x1r_p, resid_p, topi_p, gate_p, rank_p, cnt_p = _post_attention(
        xp, u_p, va_p, g_p, attn_p, p_prompt.reshape(tp, -1), base0, post_consts, min(spatial, seq), dn_alpha)
    x1r_s, resid_s, topi_s, gate_s, rank_s, cnt = _post_attention(
        xsm, u_s, va_s.astype(BF16), g_s, attn_s, p_sample.reshape(ts, -1), cnt_p, post_consts,
        min(spatial, new), dn_alpha)

    counts = cnt[0, :N_EXPERTS].astype(I32)
    padded = (counts + EXPERT_BLOCK - 1) // EXPERT_BLOCK * EXPERT_BLOCK
    pad_end = jnp.cumsum(padded)
    pad_start = (pad_end - padded).astype(I32)
    n_assign = (tp + ts) * TOP_K
    nblk = -(-n_assign // EXPERT_BLOCK) + N_EXPERTS
    n_rows = nblk * EXPERT_BLOCK
    block_row0 = jnp.arange(nblk, dtype=I32) * EXPERT_BLOCK
    block_expert = jnp.minimum(jnp.sum(pad_end[None, :] <= block_row0[:, None], axis=1), N_EXPERTS - 1).astype(I32)
    n_used = (pad_end[-1:] // EXPERT_BLOCK).astype(I32)

    dest_of = lambda topi, rank: (jnp.take(pad_start, topi[:, :TOP_K]) + rank[:, :TOP_K]).reshape(-1)
    dest_p, dest_s = dest_of(topi_p, rank_p), dest_of(topi_s, rank_s)
    xs = _dispatch(pad_start + counts, padded - counts, n_used, x1r_p, x1r_s,
                   jnp.concatenate([dest_p, dest_s]), n_rows)

    ys = _experts(block_expert, n_used, xs, w_gu.astype(F32), b_gu.astype(F32)[:, None, :],
                  w_dn.astype(F32), b_dn.astype(F32)[:, None, :])

    l2g, l2b = vec(ln2_g), vec(ln2_b)
    y_p = _combine(resid_p, gate_p, dest_p, ys, l2g, l2b)
    y_s = _combine(resid_s, gate_s, dest_s, ys, l2g, l2b)

    shp = lambda a, b_, n: a.reshape(b_, n, N_HEADS, HEAD_DIM)
    rows_first = lambda a: jnp.transpose(a, (0, 3, 1, 2))
    return (y_p.reshape(nb, seq, d), y_s.reshape(sb, new, d),
            rows_first(k_p), rows_first(v_p), lf_p.reshape(nb, seq, N_HEADS),
            shp(k_s, sb, new), shp(v_s, sb, new), lf_s[:, :N_HEADS].reshape(sb, new, N_HEADS),
            va_s.reshape(sb, new, N_GROUPS, d // N_GROUPS))


def kernel(x_prompt, x_sample, cache_fox_k, cache_fox_v, cache_fox_logf, p_prompt, p_sample, w_in, b_in, lnv_g, lnv_b, w_s, b_s, w_pa, w_pb, w_o, ln1_g, ln1_b, w_router, b_router, w_gu, b_gu, w_dn, b_dn, w_ple_gate, b_ple_gate, w_ple_proj, ln2_g, ln2_b):
    depth = w_in.shape[0]
    assert depth == 1, "the layer loop below carries one layer"
    dn_alpha = float((2 * depth) ** 0.25)
    outs = _layer(x_prompt, x_sample, cache_fox_k[0], cache_fox_v[0], cache_fox_logf[0], p_prompt[0], p_sample[0],
                  w_in[0], b_in[0], lnv_g[0], lnv_b[0], w_s[0], b_s[0], w_pa[0], w_pb[0], w_o[0], ln1_g[0], ln1_b[0],
                  w_router[0], b_router[0], w_gu[0], b_gu[0], w_dn[0], b_dn[0], w_ple_gate[0], b_ple_gate[0],
                  w_ple_proj[0], ln2_g[0], ln2_b[0], dn_alpha)
    y_p, y_s, k_p, v_p, lf_p, k_s, v_s, lf_s, va_s = outs
    lead = lambda a: a[None]
    return (y_p, y_s, lead(k_p), lead(v_p), lead(lf_p), lead(k_s), lead(v_s), lead(lf_s), lead(va_s))
```

```python
import functools

import jax
import jax.numpy as jnp
import numpy as np
from jax import lax
from jax.experimental import pallas as pl
from jax.experimental.pallas import tpu as pltpu

F32 = jnp.float32
BF16 = jnp.bfloat16
I32 = jnp.int32

LANES = 128
SUBLANES = 8
D_MODEL = 1024
N_HEADS = 16
HEAD_DIM = 64
N_GROUPS = 8
N_EXPERTS = 32
TOP_K = 4
SWIGLU_ALPHA = 1.702
SWIGLU_LIMIT = 7.0
LN_EPS = 1e-5
NEG_BIG = -1e30
LOG2_E = 1.4426950408889634
EXP_ROWS = 32
DEN_ROWS = 16

TOKEN_TILE = 256
Q_TILE = 512
ATTN_HEADS = 2
SAMPLE_CHUNK = 2048
EXPERT_BLOCK = 512
DISPATCH_TILE = 1024
COMBINE_TILE = 256
DMA_UNROLL = 8
VMEM_LIMIT = 56 * 1024 * 1024

F_MID_SHIFT = N_HEADS
F_LO_SHIFT = 2 * N_HEADS
F_ONE_LANE = 3 * N_HEADS


def _dot(a, b):
    return jnp.dot(a, b, preferred_element_type=F32)


def _dot_nt(a, b):
    return lax.dot_general(a, b, (((1,), (1,)), ((), ())), preferred_element_type=F32)


def _layer_norm(x, g, b):
    mu = jnp.mean(x, axis=-1, keepdims=True)
    xc = x - mu
    var = jnp.mean(xc * xc, axis=-1, keepdims=True)
    return xc * lax.rsqrt(var + LN_EPS) * g + b


def _log_sigmoid(z):
    return jnp.minimum(z, 0.0) - jnp.log1p(jnp.exp(-jnp.abs(z)))


def _split3(x):
    hi = x.astype(BF16).astype(F32)
    r = x - hi
    mid = r.astype(BF16).astype(F32)
    lo = (r - mid).astype(BF16).astype(F32)
    return hi, mid, lo


def _tri(n, strict):
    r = lax.broadcasted_iota(I32, (n, n), 0)
    c = lax.broadcasted_iota(I32, (n, n), 1)
    keep = (r > c) if strict else (r >= c)
    return jnp.where(keep, 1.0, 0.0).astype(F32)


def _cumsum_rows(x):
    n = x.shape[0]
    hi, mid, lo = _split3(x)
    cat = jnp.concatenate([hi, mid, lo], axis=1).astype(BF16)
    c = _dot(_tri(n, strict=False).astype(BF16), cat)
    return (c[:, 2 * LANES:] + c[:, LANES:2 * LANES]) + c[:, :LANES]


def _proj_sections(x_ref, wuv, wkv, wf, wg, buv, bkv, bf_, bg, lng, lnb, u_o, g_o):
    xb = x_ref[...].astype(BF16)
    pu = _dot(xb, wuv[:, :D_MODEL]) + buv[:, :D_MODEL]
    u_o[...] = jax.nn.gelu(pu).astype(BF16)
    pv = _dot(xb, wuv[:, D_MODEL:]) + buv[:, D_MODEL:]
    va = _layer_norm(jax.nn.gelu(pv), lng[...], lnb[...])
    k = _dot(xb, wkv[:, :D_MODEL]) + bkv[:, :D_MODEL]
    v = _dot(xb, wkv[:, D_MODEL:]) + bkv[:, D_MODEL:]
    g_o[...] = jax.nn.sigmoid(_dot(xb, wg[...]) + bg[...]).astype(BF16)
    f = _dot(xb, wf[...]) + bf_[...]
    lane = lax.broadcasted_iota(I32, f.shape, 1)
    lf = jnp.where(lane < N_HEADS, _log_sigmoid(f), 0.0)
    return xb, va, k, v, lf


def _proj_prompt_kernel(x_ref, wuv, wq, wkv, wf, wg, buv, bq, bkv, bf_, bg, lng, lnb, eq, ek,
                        u_o, va_o, k_o, v_o, lf_o, g_o, qa_o, ka_o, vt_o, carry, *, tiles_per_seq):
    i = pl.program_id(0)
    xb, va, k, v, lf = _proj_sections(x_ref, wuv, wkv, wf, wg, buv, bkv, bf_, bg, lng, lnb, u_o, g_o)
    va_o[...] = va.astype(BF16)
    v_t = v.T
    k_o[0] = k.T.reshape(k_o.shape[1:])
    v_o[0] = v_t.reshape(v_o.shape[1:])
    vt_o[0, 0] = v_t.astype(BF16)
    lf_o[...] = lf[:, :N_HEADS]

    @pl.when(i % tiles_per_seq == 0)
    def _():
        carry[...] = jnp.zeros_like(carry)

    tm = lf.shape[0]
    fcum = carry[...] + _cumsum_rows(lf)
    carry[...] = fcum[tm - 1:tm, :]

    hi, mid, lo = _split3(fcum * LOG2_E)
    lane = lax.broadcasted_iota(I32, fcum.shape, 1)
    pack = (hi + pltpu.roll(mid, F_MID_SHIFT, 1) + pltpu.roll(lo, F_LO_SHIFT, 1)
            + jnp.where(lane == F_ONE_LANE, 1.0, 0.0)).astype(BF16)
    add_q = _dot(pack, eq[...])
    add_k = _dot(pack, ek[...])
    q = (_dot(xb, wq[...]) + bq[...]) * (HEAD_DIM ** -0.5 * LOG2_E)
    low =lane < HEAD_DIM
    for g in range(N_GROUPS):
        sl = slice(g * LANES, (g + 1) * LANES)
        ev = slice(2 * g * LANES, (2 * g + 1) * LANES)
        od = slice((2 * g + 1) * LANES, (2 * g + 2) * LANES)
        qa_o[:, ev] = jnp.where(low, q[:, sl], add_q[:, ev]).astype(BF16)
        qa_o[:, od] = jnp.where(low, add_q[:, od], q[:, sl]).astype(BF16)
        ka_o[:, ev] = jnp.where(low, k[:, sl], add_k[:, ev]).astype(BF16)
        ka_o[:, od] = jnp.where(low, add_k[:, od], k[:, sl]).astype(BF16)


def _proj_sample_kernel(x_ref, wuv, wq, wkv, wf, wg, buv, bq, bkv, bf_, bg, lng, lnb,
                        u_o, va_o, k_o, v_o, lf_o, g_o, q_o):
    xb, va, k, v, lf = _proj_sections(x_ref, wuv, wkv, wf, wg, buv, bkv, bf_, bg, lng, lnb, u_o, g_o)
    k_o[...] = k
    v_o[...] = v
    va_o[...] = va
    lf_o[...] = lf
    q_o[...] = (_dot(xb, wq[...]) + bq[...]) * (HEAD_DIM ** -0.5)


def _const_spec(shape):
    nd = len(shape)
    return pl.BlockSpec(shape, lambda *_: (0,) * nd, pipeline_mode=pl.Buffered(1))


def _proj_weights(w_in, b_in):
    d = D_MODEL
    off_q, off_k, off_f = 2 * d, 3 * d, 5 * d
    off_ga = off_f + N_HEADS
    wuv = w_in[:, :off_q].astype(BF16)
    wq = w_in[:, off_q:off_k].astype(BF16)
    wkv = w_in[:, off_k:off_f].astype(BF16)
    wf = jnp.pad(w_in[:, off_f:off_ga], ((0, 0), (0, LANES - N_HEADS))).astype(BF16)
    wg = w_in[:, off_ga:].astype(BF16)
    b = b_in.reshape(1, -1).astype(F32)
    buv, bq, bkv = b[:, :off_q], b[:, off_q:off_k], b[:, off_k:off_f]
    bf_ = jnp.pad(b[:, off_f:off_ga], ((0, 0), (0, LANES - N_HEADS)))
    bg = b[:, off_ga:]
    return (wuv, wq, wkv, wf, wg), (buv, bq, bkv, bf_, bg)


def _spread_matrices():
    eq_np = np.zeros((LANES, 2 * D_MODEL), np.float32)
    ek_np = np.zeros((LANES, 2 * D_MODEL), np.float32)
    for h in range(N_HEADS):
        base = LANES * h + (HEAD_DIM if h % 2 == 0 else 0)
        for c, shift in enumerate((0, F_MID_SHIFT, F_LO_SHIFT)):
            eq_np[shift + h, base + c] = 1.0
            eq_np[F_ONE_LANE, base + 3 + c] = 1.0
            ek_np[F_ONE_LANE, base + c] = 1.0
            ek_np[shift + h, base + 3 + c] = -1.0
    return jnp.asarray(eq_np, BF16), jnp.asarray(ek_np, BF16)


def _project_prompt(x2d, seq, ws, bs, lng, lnb):
    t = x2d.shape[0]
    tm = min(TOKEN_TILE, seq)
    nt = t // tm
    nb = t // seq
    eq, ek = _spread_matrices()
    consts = list(ws) + list(bs) + [lng, lnb, eq, ek]
    row = lambda w: pl.BlockSpec((tm, w), lambda i: (i, 0))
    out_shape = (
        jax.ShapeDtypeStruct((t, D_MODEL), BF16),
        jax.ShapeDtypeStruct((t, D_MODEL), BF16),
        jax.ShapeDtypeStruct((nb, N_HEADS, HEAD_DIM, seq), F32),
        jax.ShapeDtypeStruct((nb, N_HEADS, HEAD_DIM, seq), F32),
        jax.ShapeDtypeStruct((t, N_HEADS), F32),
        jax.ShapeDtypeStruct((t, 2 * D_MODEL), BF16),
        jax.ShapeDtypeStruct((t, 2 * D_MODEL), BF16),
        jax.ShapeDtypeStruct((t, 2 * D_MODEL), BF16),
        jax.ShapeDtypeStruct((nb, seq // tm, D_MODEL, tm), BF16),
    )
    tiles_per_seq = seq // tm
    heads = pl.BlockSpec((1, N_HEADS, HEAD_DIM, tm), lambda i: (i // tiles_per_seq, 0, 0, i % tiles_per_seq))
    out_specs = (row(D_MODEL), row(D_MODEL), heads, heads, row(N_HEADS),
                 row(2 * D_MODEL), row(2 * D_MODEL), row(2 * D_MODEL),
                 pl.BlockSpec((1, 1, D_MODEL, tm), lambda i: (i // tiles_per_seq, i % tiles_per_seq, 0, 0)))
    return pl.pallas_call(
        functools.partial(_proj_prompt_kernel, tiles_per_seq=tiles_per_seq),
        grid=(nt,),
        in_specs=[row(D_MODEL)] + [_const_spec(c.shape) for c in consts],
        out_specs=out_specs,
        out_shape=out_shape,
        scratch_shapes=[pltpu.VMEM((1, LANES), F32)],
        compiler_params=pltpu.CompilerParams(dimension_semantics=("arbitrary",), vmem_limit_bytes=VMEM_LIMIT),
        name="proj_prompt",
    )(x2d, *consts)


def _project_sample(x2d, ws, bs, lng, lnb):
    t = x2d.shape[0]
    tm = min(TOKEN_TILE, t)
    nt = t // tm
    consts = list(ws) + list(bs) + [lng, lnb]
    row = lambda w: pl.BlockSpec((tm, w), lambda i: (i, 0))
    out_shape = (
        jax.ShapeDtypeStruct((t, D_MODEL), BF16),
        jax.ShapeDtypeStruct((t, D_MODEL), F32),
        jax.ShapeDtypeStruct((t, D_MODEL), F32),
        jax.ShapeDtypeStruct((t, D_MODEL), F32),
        jax.ShapeDtypeStruct((t, LANES), F32),
        jax.ShapeDtypeStruct((t, 2 * D_MODEL), BF16),
        jax.ShapeDtypeStruct((t, D_MODEL), F32),
    )
    out_specs = (row(D_MODEL), row(D_MODEL), row(D_MODEL), row(D_MODEL), row(LANES), row(2 * D_MODEL), row(D_MODEL))
    return pl.pallas_call(
        _proj_sample_kernel,
        grid=(nt,),
        in_specs=[row(D_MODEL)] + [_const_spec(c.shape) for c in consts],
        out_specs=out_specs,
        out_shape=out_shape,
        compiler_params=pltpu.CompilerParams(dimension_semantics=("arbitrary",), vmem_limit_bytes=VMEM_LIMIT),
        name="proj_sample",
    )(x2d, *consts)


def _attn_prompt_kernel(q_ref, k_ref, vt_ref, o_ref, m_ref, acc_ref, ot_ref,
                        s0, s1, p0, p1, a0, a1, c0, c1, qt_sc, *, tq, tk):
    qi = pl.program_id(2)
    nh = m_ref.shape[0]
    assert tq == 2 * tk, "the pipeline below is written for two diagonal spans per query tile"
    m_ref[...] = jnp.full(m_ref.shape, NEG_BIG, F32)
    acc_ref[...] = jnp.zeros(acc_ref.shape, F32)
    p1[...] = jnp.zeros(p1.shape, BF16)
    a1[...] = jnp.ones(a1.shape, F32)
    tv = vt_ref.shape[3]
    ones = jnp.ones((DEN_ROWS, tv), BF16)
    lanes = lambda hh: slice(hh * LANES, (hh + 1) * LANES)

    def score(t, s_slot, c_slot, transposed_q=True):
        start = pl.multiple_of(t * tk, tk)
        for hh in range(nh):
            k = k_ref[pl.ds(start, tk), lanes(hh)]
            st = _dot(k, qt_sc[hh]) if transposed_q else _dot_nt(k, q_ref[:, lanes(hh)])
            s_slot[hh] = st
            c_slot[hh] = jnp.max(st, axis=0, keepdims=True)

    def softmax(s_slot, c_slot, p_slot, a_slot, mask_offset):
        for hh in range(nh):
            if mask_offset is None:
                col_max = c_slot[hh]
            else:
                key_row = lax.broadcasted_iota(I32, (tk, tq), 0)
                qry_col = lax.broadcasted_iota(I32, (tk, tq), 1)
                s_slot[hh] = jnp.where(key_row + mask_offset <= qry_col, s_slot[hh], NEG_BIG)
                col_max = jnp.max(s_slot[hh], axis=0, keepdims=True)
            m_old = m_ref[hh]
            m_new = jnp.maximum(m_old, col_max)
            a_slot[hh] = jnp.exp2(m_old - m_new)
            for r in range(0, tk, EXP_ROWS):
                rows = slice(r, r + EXP_ROWS)
                p_slot[hh, rows, :] = jnp.exp2((s_slot[hh, rows, :] - m_new).astype(BF16))
            m_ref[hh] = m_new

    def value(t, p_slot, a_slot):
        span = jnp.maximum(t, 0)
        for hh in range(nh):
            pv = None
            for d in range(tk // tv):
                vt = vt_ref[0, span * (tk // tv) + d, hh * HEAD_DIM:(hh + 1) * HEAD_DIM, :]
                part = _dot(jnp.concatenate([vt, ones], axis=0), p_slot[hh, d * tv:(d + 1) * tv, :])
                pv = part if pv is None else pv + part
            acc_ref[hh] = a_slot[hh] * acc_ref[hh] + pv

    score(0, s0, c0, transposed_q=False)
    for hh in range(nh):
        qt_sc[hh] = q_ref[:, lanes(hh)].astype(F32).T.astype(BF16)

    def two_steps(ii):
        t = 2 * ii + 1
        value(t - 2, p1, a1)
        score(t, s1, c1)
        softmax(s0, c0, p0, a0, None)
        value(t - 1, p0, a0)
        score(t + 1, s0, c0)
        softmax(s1, c1, p1, a1, None)

    def body(jj, c):
        two_steps(2 * jj)
        two_steps(2 * jj + 1)
        return c

    lax.fori_loop(0, qi // 2, body, 0)

    @pl.when(qi % 2 == 1)
    def _():
        two_steps(qi - 1)

    t = 2 * qi + 1
    value(t - 2, p1, a1)
    score(t, s1, c1)
    softmax(s0, c0, p0, a0, 0)
    value(t - 1, p0, a0)
    softmax(s1, c1, p1, a1, tk)
    value(t, p1, a1)
    for hh in range(nh):
        ot_ref[hh * HEAD_DIM:(hh + 1) * HEAD_DIM, :] = acc_ref[hh, :HEAD_DIM] / acc_ref[hh, HEAD_DIM:HEAD_DIM + 1]
    o_ref[...] = ot_ref[...].T.astype(BF16)


def _attend_prompt(q_aug, k_aug, v_t, nb, seq):
    t = q_aug.shape[0]
    tv = v_t.shape[3]
    tq = min(Q_TILE, seq)
    tk = tq // 2
    nq = seq // tq
    nh = ATTN_HEADS
    return pl.pallas_call(
        functools.partial(_attn_prompt_kernel, tq=tq, tk=tk),
        grid=(nb, N_HEADS // nh, nq),
        in_specs=[
            pl.BlockSpec((tq, nh * LANES), lambda b, g, i: (b * nq + i, g)),
            pl.BlockSpec((seq, nh * LANES), lambda b, g, i: (b, g)),
            pl.BlockSpec((1, seq // tv, nh * HEAD_DIM, tv), lambda b, g, i: (b, 0, g, 0)),
        ],
        out_specs=pl.BlockSpec((tq, nh * HEAD_DIM), lambda b, g, i: (b * nq + i, g)),
        out_shape=jax.ShapeDtypeStruct((t, D_MODEL), BF16),
        scratch_shapes=[pltpu.VMEM((nh, 1, tq), F32), pltpu.VMEM((nh, HEAD_DIM + DEN_ROWS, tq), F32),
                        pltpu.VMEM((nh * HEAD_DIM, tq), F32),
                        pltpu.VMEM((nh, tk, tq), F32), pltpu.VMEM((nh, tk, tq), F32),
                        pltpu.VMEM((nh, tk, tq), BF16), pltpu.VMEM((nh, tk, tq), BF16),
                        pltpu.VMEM((nh, 1, tq), F32), pltpu.VMEM((nh, 1, tq), F32),
                        pltpu.VMEM((nh, 1, tq), F32), pltpu.VMEM((nh, 1, tq), F32),
                        pltpu.VMEM((nh, LANES, tq), BF16)],
        compiler_params=pltpu.CompilerParams(
            dimension_semantics=("arbitrary", "arbitrary", "arbitrary"), vmem_limit_bytes=VMEM_LIMIT),
        name="attn_prompt",
    )(q_aug, k_aug, v_t)


def _attn_sample_kernel(q_ref, k_ref, v_ref, lf_ref, ck_ref, cv_ref, clf_ref, o_ref,
                        fct_sc, fn_sc, m_sc, l_sc, acc_sc, *, n_chunks):
    c = pl.program_id(1)
    chunk = ck_ref.shape[3]
    new = q_ref.shape[1]

    @pl.when(c == 0)
    def _():
        carry = jnp.zeros((1, LANES), F32)
        for cc in range(n_chunks):
            fc = carry + _cumsum_rows(clf_ref[0, cc * chunk:(cc + 1) * chunk, :])
            carry = fc[chunk - 1:chunk, :]
            fct_sc[cc] = fc.T
        fn_sc[...] = carry + _cumsum_rows(lf_ref[0])
        m_sc[...] = jnp.full(m_sc.shape, NEG_BIG, F32)
        l_sc[...] = jnp.zeros(l_sc.shape, F32)
        acc_sc[...] = jnp.zeros(acc_sc.shape, F32)

    lane = lax.broadcasted_iota(I32, (new, LANES), 1)
    f_new = fn_sc[...]

    def attend(score, weigh, f_keys_t, mask):
        for g in range(N_GROUPS):
            qg = q_ref[0, :, g * LANES:(g + 1) * LANES]
            for hh in range(2):
                h = 2 * g + hh
                mine = (lane < HEAD_DIM) if hh == 0 else (lane >= HEAD_DIM)
                qm = jnp.where(mine, qg, 0.0).astype(BF16)
                s = score(qm, g) + f_new[:, h:h + 1] - f_keys_t[h:h + 1, :]
                if mask is not None:
                    s = jnp.where(mask, s, NEG_BIG)
                m_old = m_sc[h]
                m_new = jnp.maximum(m_old, jnp.max(s, axis=1, keepdims=True))
                alpha = jnp.exp(m_old - m_new)
                p = jnp.exp(s - m_new)
                l_sc[h] = alpha * l_sc[h] + jnp.sum(p, axis=1, keepdims=True)
                acc_sc[h] = alpha * acc_sc[h] + weigh(p.astype(BF16), g)
                m_sc[h] = m_new

    pair_t = lambda ref, g: ref[0, 2 * g:2 * g + 2].reshape(LANES, chunk).astype(BF16)
    attend(lambda qm, g: _dot(qm, pair_t(ck_ref, g)), lambda p, g: _dot_nt(p, pair_t(cv_ref, g)), fct_sc[c], None)

    @pl.when(c == n_chunks - 1)
    def _():
        causal = lax.broadcasted_iota(I32, (new, new), 1) <= lax.broadcasted_iota(I32, (new, new), 0)
        cols = lambda ref, g: ref[0, :, g * LANES:(g + 1) * LANES].astype(BF16)
        attend(lambda qm, g: _dot_nt(qm, cols(k_ref, g)), lambda p, g: _dot(p, cols(v_ref, g)), f_new.T, causal)
        for g in range(N_GROUPS):
            even = acc_sc[2 * g] / l_sc[2 * g]
            odd = acc_sc[2 * g + 1] / l_sc[2 * g + 1]
            o_ref[:, g * LANES:(g + 1) * LANES] = jnp.where(lane < HEAD_DIM, even, odd).astype(BF16)


def _attend_sample(q, k, v, lf, cache_k, cache_v, cache_lf):
    nb, new, _ = q.shape
    past = cache_k.shape[3]
    chunk = min(SAMPLE_CHUNK, past)
    n_chunks = past // chunk
    blk = lambda n, w: pl.BlockSpec((1, n, w), lambda b, c: (b, 0, 0))
    cache = pl.BlockSpec((1, N_HEADS, HEAD_DIM, chunk), lambda b, c: (b, 0, 0, c))
    return pl.pallas_call(
        functools.partial(_attn_sample_kernel, n_chunks=n_chunks),
        grid=(nb, n_chunks),
        in_specs=[blk(new, D_MODEL), blk(new, D_MODEL), blk(new, D_MODEL), blk(new, LANES),
                  cache, cache, blk(past, LANES)],
        out_specs=pl.BlockSpec((new, D_MODEL), lambda b, c: (b, 0)),
        out_shape=jax.ShapeDtypeStruct((nb * new, D_MODEL), BF16),
        scratch_shapes=[pltpu.VMEM((n_chunks, LANES, chunk), F32), pltpu.VMEM((new, LANES), F32),
                        pltpu.VMEM((N_HEADS, new, 1), F32), pltpu.VMEM((N_HEADS, new, 1), F32),
                        pltpu.VMEM((N_HEADS, new, LANES), F32)],
        compiler_params=pltpu.CompilerParams(dimension_semantics=("arbitrary", "arbitrary"),
                                             vmem_limit_bytes=VMEM_LIMIT),
        name="attn_sample",
    )(q, k, v, lf, cache_k, cache_v, cache_lf)


def _post_kernel(x_ref, u_ref, va_ref, g_ref, at_ref, p_ref, base_ref, ws_ref, bst_ref, wpa, wpb, wo,
                 l1g, l1b, wr_hi, wr_lo, br, wpg, bpg, wpp,
                 x1_o, resid_o, topi_o, gate_o, rank_o, cnt_o, a_sc, base_sc, *, chunk, dn_alpha):
    i = pl.program_id(0)
    tm = x_ref.shape[0]

    @pl.when(i == 0)
    def _():
        base_sc[...] = base_ref[...]

    tri = _tri(chunk, strict=False)
    for g in range(N_GROUPS):
        sl = slice(g * LANES, (g + 1) * LANES)
        wm = (ws_ref[g, :chunk, :chunk] * tri).astype(BF16)
        bias = bst_ref[:chunk, g:g + 1]
        for c in range(tm // chunk):
            rows = slice(c * chunk, (c + 1) * chunk)
            sg = _dot(wm, va_ref[rows, sl]) + bias
            a_sc[rows, sl] = (u_ref[rows, sl].astype(F32) * sg).astype(BF16)

    pa = _dot(a_sc[...], wpa[...])
    pb = _dot(at_ref[...], wpb[...])
    merged = g_ref[:, :D_MODEL].astype(F32) * pa + g_ref[:, D_MODEL:].astype(F32) * pb
    x1 = _layer_norm(dn_alpha * x_ref[...] + _dot(merged.astype(BF16), wo[...]), l1g[...], l1b[...])
    x1b = x1.astype(BF16)
    x1_o[...] = x1

    ple = jax.nn.sigmoid(_dot(x1b, wpg[...]) + bpg[...]) * _dot(p_ref[...].astype(BF16), wpp[...])
    resid_o[...] = dn_alpha * x1 + ple

    x1_lo = (x1 - x1b.astype(F32)).astype(BF16)
    logits = (_dot(x1b, wr_hi[...]) + (_dot(x1b, wr_lo[...]) + _dot(x1_lo, wr_hi[...]))) + br[...]
    lane = lax.broadcasted_iota(I32, (tm, LANES), 1)
    work = logits
    vals, idxs = [], []
    for _ in range(TOP_K):
        mk = jnp.max(work, axis=1, keepdims=True)
        ik = jnp.min(jnp.where(work == mk, lane, LANES), axis=1, keepdims=True)
        vals.append(mk)
        idxs.append(ik)
        work = jnp.where(lane == ik, -jnp.inf, work)
    exps = [jnp.exp(v - vals[0]) for v in vals]
    den = exps[0] + exps[1] + exps[2] + exps[3]

    lower = _tri(tm, strict=True).astype(BF16)
    run = base_sc[...]
    topi = jnp.zeros((tm, LANES), I32)
    gate = jnp.zeros((tm, LANES), F32)
    rank = jnp.zeros((tm, LANES), F32)
    for k in range(TOP_K):
        onehot = jnp.where(lane == idxs[k], 1.0, 0.0)
        before = _dot(lower, onehot.astype(BF16))
        rk = jnp.sum(onehot * (before + run), axis=1, keepdims=True)
        topi = jnp.where(lane == k, idxs[k], topi)
        gate = jnp.where(lane == k, exps[k] / den, gate)
        rank = jnp.where(lane == k, rk, rank)
        run = run + jnp.sum(onehot, axis=0, keepdims=True)
    base_sc[...] = run
    topi_o[...] = topi
    gate_o[...] = gate
    rank_o[...] = rank.astype(I32)
    cnt_o[...] = run


def _post_attention(x2d, u, va, gates, attn, p2d, base, consts, chunk, dn_alpha):
    t = x2d.shape[0]
    tm = min(TOKEN_TILE, t)
    nt = t // tm
    row = lambda w: pl.BlockSpec((tm, w), lambda i: (i, 0))
    ple_dim = p2d.shape[1]
    out_shape = (
        jax.ShapeDtypeStruct((t, D_MODEL), F32),
        jax.ShapeDtypeStruct((t, D_MODEL), F32),
        jax.ShapeDtypeStruct((t, LANES), I32),
        jax.ShapeDtypeStruct((t, LANES), F32),
        jax.ShapeDtypeStruct((t, LANES), I32),
        jax.ShapeDtypeStruct((1, LANES), F32),
    )
    out_specs = (row(D_MODEL), row(D_MODEL), row(LANES), row(LANES), row(LANES),
                 pl.BlockSpec((1, LANES), lambda i: (0, 0)))
    return pl.pallas_call(
        functools.partial(_post_kernel, chunk=chunk, dn_alpha=dn_alpha),
        grid=(nt,),
        in_specs=[row(D_MODEL), row(D_MODEL), row(D_MODEL), row(2 * D_MODEL), row(D_MODEL), row(ple_dim),
                  _const_spec(base.shape)] + [_const_spec(c.shape) for c in consts],
        out_specs=out_specs,
        out_shape=out_shape,
        scratch_shapes=[pltpu.VMEM((tm, D_MODEL), BF16), pltpu.VMEM((1, LANES), F32)],
        compiler_params=pltpu.CompilerParams(dimension_semantics=("arbitrary",), vmem_limit_bytes=VMEM_LIMIT),
        name="post_attention",
    )(x2d, u, va, gates, attn, p2d, base, *consts)


def _dispatch_kernel(fill_start_ref, fill_len_ref, nu_ref, xp_ref, xn_ref, dest_ref, xs_out, zero_sc, sem, zsem,
                     *, prompt_steps, n_blocks):
    i = pl.program_id(0)
    td = xp_ref.shape[0]

    @pl.when(i == 0)
    def _():
        zero_sc[...] = jnp.zeros(zero_sc.shape, zero_sc.dtype)

        def zero_fill(wait):
            def go(copy):
                copy.wait() if wait else copy.start()

            def per_expert(e, c):
                n = fill_len_ref[e]
                base = fill_start_ref[e]
                head = jnp.bitwise_and(n, SUBLANES - 1)
                for r in range(SUBLANES - 1):
                    @pl.when(r < head)
                    def _(r=r):
                        go(pltpu.make_async_copy(zero_sc.at[pl.ds(0, 1)], xs_out.at[pl.ds(base + r, 1)], zsem))

                piece = EXPERT_BLOCK // 2
                while piece >= SUBLANES:
                    offset = pl.multiple_of(base + head + jnp.bitwise_and(n, -(2 * piece)), SUBLANES)

                    @pl.when(jnp.bitwise_and(n, piece) != 0)
                    def _(piece=piece, offset=offset):
                        go(pltpu.make_async_copy(zero_sc.at[pl.ds(0, piece)], xs_out.at[pl.ds(offset, piece)], zsem))

                    piece //= 2
                return c

            lax.fori_loop(0, N_EXPERTS, per_expert, 0)

            def per_block(j, c):
                row0 = pl.multiple_of(j * EXPERT_BLOCK, EXPERT_BLOCK)
                go(pltpu.make_async_copy(zero_sc, xs_out.at[pl.ds(row0, EXPERT_BLOCK)], zsem))
                return c

            lax.fori_loop(nu_ref[0], n_blocks, per_block, 0)

        zero_fill(wait=False)
        zero_fill(wait=True)

    def scatter(x_ref):
        def body(j, c):
            for u in range(DMA_UNROLL):
                tok = j * (DMA_UNROLL // TOP_K) + u // TOP_K
                pltpu.make_async_copy(x_ref.at[pl.ds(tok, 1)], xs_out.at[pl.ds(dest_ref[j * DMA_UNROLL + u], 1)],
                                      sem).start(priority=u % 2)
            return c

        lax.fori_loop(0, td * TOP_K // DMA_UNROLL, body, 0)
        for _ in range(TOP_K):
            pltpu.make_async_copy(x_ref, xs_out.at[pl.ds(0, td)], sem).wait()

    @pl.when(i < prompt_steps)
    def _():
        scatter(xp_ref)

    @pl.when(i >= prompt_steps)
    def _():
        scatter(xn_ref)


def _dispatch(fill_start, fill_len, n_used, x1_prompt, x1_sample, dest, n_rows):
    tp, w = x1_prompt.shape
    ts = x1_sample.shape[0]
    td = min(DISPATCH_TILE, ts)
    prompt_steps, sample_steps = tp // td, ts // td
    grid_spec = pltpu.PrefetchScalarGridSpec(
        num_scalar_prefetch=3,
        grid=(prompt_steps + sample_steps,),
        in_specs=[pl.BlockSpec((td, w), lambda i, *_: (jnp.minimum(i, prompt_steps - 1), 0)),
                  pl.BlockSpec((td, w), lambda i, *_: (jnp.maximum(i - prompt_steps, 0), 0)),
                  pl.BlockSpec((td * TOP_K,), lambda i, *_: (i,), memory_space=pltpu.SMEM)],
        out_specs=pl.BlockSpec(memory_space=pl.ANY),
        scratch_shapes=[pltpu.VMEM((EXPERT_BLOCK, w), x1_prompt.dtype),
                        pltpu.SemaphoreType.DMA(()), pltpu.SemaphoreType.DMA(())],
    )
    return pl.pallas_call(
        functools.partial(_dispatch_kernel, prompt_steps=prompt_steps, n_blocks=n_rows // EXPERT_BLOCK),
        grid_spec=grid_spec,
        out_shape=jax.ShapeDtypeStruct((n_rows, w), x1_prompt.dtype),
        compiler_params=pltpu.CompilerParams(dimension_semantics=("arbitrary",), vmem_limit_bytes=VMEM_LIMIT),
        name="dispatch",
    )(fill_start, fill_len, n_used, x1_prompt, x1_sample, dest)


def _expert_kernel(be_ref, nu_ref, next_ref, slot_ref, xs_ref, wgu, bgu, wdn, bdn, ys_o,
                   wgu_in, wdn_in, wgu_bf, wdn_bf, sem):
    j = pl.program_id(0)
    used = j < nu_ref[0]
    new_expert = jnp.logical_or(j == 0, be_ref[j] != be_ref[jnp.maximum(j - 1, 0)])
    slot = slot_ref[j]

    def weight_copies(e, s):
        return (pltpu.make_async_copy(wgu.at[e], wgu_in.at[s], sem.at[0, s]),
                pltpu.make_async_copy(wdn.at[e], wdn_in.at[s], sem.at[1, s]))

    @pl.when(j == 0)
    def _():
        for cp in weight_copies(be_ref[0], slot):
            cp.start()

    @pl.when(jnp.logical_not(used))
    def _():
        ys_o[...] = jnp.zeros(ys_o.shape, ys_o.dtype)

    @pl.when(jnp.logical_and(used, new_expert))
    def _():
        for cp in weight_copies(be_ref[j], slot):
            cp.wait()
        wgu_bf[...] = wgu_in[slot].astype(BF16)
        wdn_bf[...] = wdn_in[slot].astype(BF16)
        nxt = next_ref[j]

        @pl.when(nxt >= 0)
        def _():
            for cp in weight_copies(nxt, 1 - slot):
                cp.start()

    @pl.when(used)
    def _():
        h = _dot(xs_ref[...].astype(BF16), wgu_bf[...]) + bgu[0]
        d_e = h.shape[1] // 2
        gate = jnp.minimum(h[:, :d_e], SWIGLU_LIMIT)
        up = jnp.clip(h[:, d_e:], -SWIGLU_LIMIT, SWIGLU_LIMIT)
        glu = gate * jax.nn.sigmoid(gate * SWIGLU_ALPHA)
        ys_o[...] = _dot(((up + 1.0) * glu).astype(BF16), wdn_bf[...]) + bdn[0]


def _experts(block_expert, n_used, next_expert, block_slot, xs, wgu, bgu, wdn, bdn):
    rows, w = xs.shape
    nblk = rows // EXPERT_BLOCK
    d_in, d_gu = wgu.shape[1], wgu.shape[2]
    d_e, d_out = wdn.shape[1], wdn.shape[2]
    grid_spec = pltpu.PrefetchScalarGridSpec(
        num_scalar_prefetch=4,
        grid=(nblk,),
        in_specs=[
            pl.BlockSpec((EXPERT_BLOCK, w), lambda j, *_: (j, 0)),
            pl.BlockSpec(memory_space=pl.ANY),
            pl.BlockSpec((1, 1, d_gu), lambda j, be, *_: (be[j], 0, 0)),
            pl.BlockSpec(memory_space=pl.ANY),
            pl.BlockSpec((1, 1, d_out), lambda j, be, *_: (be[j], 0, 0)),
        ],
        out_specs=pl.BlockSpec((EXPERT_BLOCK, d_out), lambda j, *_: (j, 0)),
        scratch_shapes=[pltpu.VMEM((2, d_in, d_gu), F32), pltpu.VMEM((2, d_e, d_out), F32),
                        pltpu.VMEM((d_in, d_gu), BF16), pltpu.VMEM((d_e, d_out), BF16),
                        pltpu.SemaphoreType.DMA((2, 2))],
    )
    return pl.pallas_call(
        _expert_kernel,
        grid_spec=grid_spec,
        out_shape=jax.ShapeDtypeStruct((rows, d_out), F32),
        compiler_params=pltpu.CompilerParams(dimension_semantics=("arbitrary",), vmem_limit_bytes=VMEM_LIMIT),
        name="experts",
    )(block_expert, n_used, next_expert, block_slot, xs, wgu, bgu, wdn, bdn)


def _final_kernel(resid_ref, gate_ref, dest_ref, dest_next_ref, ys_ref, lng, lnb, y_o, buf, sem, *, n):
    i = pl.program_id(0)
    tg = resid_ref.shape[0]

    def gather(idx_ref, slot):
        def body(j, c):
            for u in range(DMA_UNROLL):
                tok = j * (DMA_UNROLL // TOP_K) + u // TOP_K
                pltpu.make_async_copy(ys_ref.at[pl.ds(idx_ref[j * DMA_UNROLL + u], 1)],
                                      buf.at[slot, u % TOP_K, pl.ds(tok, 1)], sem.at[slot]).start(priority=u % 2)
            return c

        lax.fori_loop(0, tg * TOP_K // DMA_UNROLL, body, 0)

    @pl.when(i == 0)
    def _():
        gather(dest_ref, 0)

    @pl.when(i + 1 < n)
    def _():
        gather(dest_next_ref, (i + 1) % 2)

    slot = i % 2
    for k in range(TOP_K):
        pltpu.make_async_copy(ys_ref.at[pl.ds(0, tg)], buf.at[slot, k], sem.at[slot]).wait()
    acc = resid_ref[...]
    for k in range(TOP_K):
        acc = acc + gate_ref[:, k:k + 1] * buf[slot, k]
    y_o[...] = _layer_norm(acc, lng[...], lnb[...])


def _combine(resid, gate, dest, ys, lng, lnb):
    t = resid.shape[0]
    tg = min(COMBINE_TILE, t)
    nt = t // tg
    row = lambda w: pl.BlockSpec((tg, w), lambda i: (i, 0))
    return pl.pallas_call(
        functools.partial(_final_kernel, n=nt),
        grid=(nt,),
        in_specs=[row(D_MODEL), row(LANES),
                  pl.BlockSpec((tg * TOP_K,), lambda i: (i,), memory_space=pltpu.SMEM),
                  pl.BlockSpec((tg * TOP_K,), lambda i: (jnp.minimum(i + 1, nt - 1),), memory_space=pltpu.SMEM),
                  pl.BlockSpec(memory_space=pl.ANY),
                  _const_spec(lng.shape), _const_spec(lnb.shape)],
        out_specs=row(D_MODEL),
        out_shape=jax.ShapeDtypeStruct((t, D_MODEL), F32),
        scratch_shapes=[pltpu.VMEM((2, TOP_K, tg, ys.shape[1]), F32), pltpu.SemaphoreType.DMA((2,))],
        compiler_params=pltpu.CompilerParams(dimension_semantics=("arbitrary",), vmem_limit_bytes=VMEM_LIMIT),
        name="combine",
    )(resid, gate, dest, dest, ys, lng, lnb)


def _layer(x_prompt, x_sample, cache_k, cache_v, cache_logf, p_prompt, p_sample,
           w_in, b_in, lnv_g, lnv_b, w_s, b_s, w_pa, w_pb, w_o, ln1_g, ln1_b,
           w_router, b_router, w_gu, b_gu, w_dn, b_dn, w_ple_gate, b_ple_gate, w_ple_proj, ln2_g, ln2_b,
           dn_alpha):
    nb, seq, d = x_prompt.shape
    sb, new, _ = x_sample.shape
    past = cache_k.shape[1]
    tp, ts = nb * seq, sb * new
    vec = lambda a: a.reshape(1, -1).astype(F32)

    ws, bs = _proj_weights(w_in, b_in)
    lng, lnb = vec(lnv_g), vec(lnv_b)
    xp = x_prompt.reshape(tp, d)
    xsm = x_sample.reshape(ts, d)
    u_p, va_p, k_p, v_p, lf_p, g_p, qa_p, ka_p, vt_p = _project_prompt(xp, seq, ws, bs, lng, lnb)
    u_s, va_s, k_s, v_s, lf_s, g_s, q_s = _project_sample(xsm, ws, bs, lng, lnb)

    attn_p = _attend_prompt(qa_p, ka_p, vt_p, nb, seq)
    cache_lf = jnp.pad(cache_logf.astype(F32), ((0, 0), (0, 0), (0, LANES - N_HEADS)))
    rows_last = lambda a: jnp.transpose(a, (0, 2, 3, 1))
    attn_s = _attend_sample(q_s.reshape(sb, new, d), k_s.reshape(sb, new, d), v_s.reshape(sb, new, d),
                            lf_s.reshape(sb, new, LANES), rows_last(cache_k), rows_last(cache_v), cache_lf)

    wr = jnp.pad(w_router.astype(F32), ((0, 0), (0, LANES - N_EXPERTS)))
    wr_hi = wr.astype(BF16)
    wr_lo = (wr - wr_hi.astype(F32)).astype(BF16)
    br = jnp.concatenate([b_router.astype(F32), jnp.full((LANES - N_EXPERTS,), NEG_BIG, F32)]).reshape(1, LANES)
    spatial = b_s.shape[1]
    bst = jnp.pad(b_s.astype(F32).T, ((0, 0), (0, LANES - b_s.shape[0])))
    post_consts = [w_s.astype(F32), bst, w_pa.astype(BF16), w_pb.astype(BF16), w_o.astype(BF16),
                   vec(ln1_g), vec(ln1_b), wr_hi, wr_lo, br,
                   w_ple_gate.astype(BF16), vec(b_ple_gate), w_ple_proj.astype(BF16)]
    base0 = jnp.zeros((1, LANES), F32)
    x1r_p, resid_p, topi_p, gate_p, rank_p, cnt_p = _post_attention(
        xp, u_p, va_p, g_p, attn_p, p_prompt.reshape(tp, -1), base0, post_consts, min(spatial, seq), dn_alpha)
    x1r_s, resid_s, topi_s, gate_s, rank_s, cnt = _post_attention(
        xsm, u_s, va_s.astype(BF16), g_s, attn_s, p_sample.reshape(ts, -1), cnt_p, post_consts,
        min(spatial, new), dn_alpha)

    counts = cnt[0, :N_EXPERTS].astype(I32)
    padded = (counts + EXPERT_BLOCK - 1) // EXPERT_BLOCK * EXPERT_BLOCK
    pad_end = jnp.cumsum(padded)
    pad_start = (pad_end - padded).astype(I32)
    n_assign = (tp + ts) * TOP_K
    nblk = -(-n_assign // EXPERT_BLOCK) + N_EXPERTS
    n_rows = nblk * EXPERT_BLOCK
    block_row0 = jnp.arange(nblk, dtype=I32) * EXPERT_BLOCK
    block_expert = jnp.minimum(jnp.sum(pad_end[None, :] <= block_row0[:, None], axis=1), N_EXPERTS - 1).astype(I32)
    n_used = (pad_end[-1:] // EXPERT_BLOCK).astype(I32)

    dest_of = lambda topi, rank: (jnp.take(pad_start, topi[:, :TOP_K]) + rank[:, :TOP_K]).reshape(-1)
    dest_p, dest_s = dest_of(topi_p, rank_p), dest_of(topi_s, rank_s)
    xs = _dispatch(pad_start + counts, padded - counts, n_used, x1r_p, x1r_s,
                   jnp.concatenate([dest_p, dest_s]), n_rows)

    expert_ids = jnp.arange(N_EXPERTS, dtype=I32)
    present = padded > 0
    later = jnp.where(jnp.logical_and(present[None, :], expert_ids[None, :] > expert_ids[:, None]),
                      expert_ids[None, :], N_EXPERTS)
    next_present = jnp.min(later, axis=1)
    next_present = jnp.where(next_present == N_EXPERTS, -1, next_present).astype(I32)
    order = (jnp.cumsum(present.astype(I32)) - 1).astype(I32)
    ys = _experts(block_expert, n_used, jnp.take(next_present, block_expert), jnp.take(order, block_expert) % 2,
                  xs, w_gu.astype(F32), b_gu.astype(F32)[:, None, :], w_dn.astype(F32), b_dn.astype(F32)[:, None, :])

    l2g, l2b = vec(ln2_g), vec(ln2_b)
    y_p = _combine(resid_p, gate_p, dest_p, ys, l2g, l2b)
    y_s = _combine(resid_s, gate_s, dest_s, ys, l2g, l2b)

    shp = lambda a, b_, n: a.reshape(b_, n, N_HEADS, HEAD_DIM)
    rows_first = lambda a: jnp.transpose(a, (0, 3, 1, 2))
    return (y_p.reshape(nb, seq, d), y_s.reshape(sb, new, d),
            rows_first(k_p), rows_first(v_p), lf_p.reshape(nb, seq, N_HEADS),
            shp(k_s, sb, new), shp(v_s, sb, new), lf_s[:, :N_HEADS].reshape(sb, new, N_HEADS),
            va_s.reshape(sb, new, N_GROUPS, d // N_GROUPS))


def kernel(x_prompt, x_sample, cache_fox_k, cache_fox_v, cache_fox_logf, p_prompt, p_sample, w_in, b_in, lnv_g, lnv_b, w_s, b_s, w_pa, w_pb, w_o, ln1_g, ln1_b, w_router, b_router, w_gu, b_gu, w_dn, b_dn, w_ple_gate, b_ple_gate, w_ple_proj, ln2_g, ln2_b):
    depth = w_in.shape[0]
    assert depth == 1, "the layer loop below carries one layer"
    dn_alpha = float((2 * depth) ** 0.25)
    outs = _layer(x_prompt, x_sample, cache_fox_k[0], cache_fox_v[0], cache_fox_logf[0], p_prompt[0], p_sample[0],
                  w_in[0], b_in[0], lnv_g[0], lnv_b[0], w_s[0], b_s[0], w_pa[0], w_pb[0], w_o[0], ln1_g[0], ln1_b[0],
                  w_router[0], b_router[0], w_gu[0], b_gu[0], w_dn[0], b_dn[0], w_ple_gate[0], b_ple_gate[0],
                  w_ple_proj[0], ln2_g[0], ln2_b[0], dn_alpha)
    y_p, y_s, k_p, v_p, lf_p, k_s, v_s, lf_s, va_s = outs
    lead = lambda a: a[None]
    return (y_p, y_s, lead(k_p), lead(v_p), lead(lf_p), lead(k_s), lead(v_s), lead(lf_s), lead(va_s))
```

```python
import functools

import jax
import jax.numpy as jnp
import numpy as np
from jax import lax
from jax.experimental import pallas as pl
from jax.experimental.pallas import tpu as pltpu

F32 = jnp.float32
BF16 = jnp.bfloat16
I32 = jnp.int32

LANES = 128
SUBLANES = 8
D_MODEL = 1024
N_HEADS = 16
HEAD_DIM = 64
N_GROUPS = 8
N_EXPERTS = 32
TOP_K = 4
SWIGLU_ALPHA = 1.702
SWIGLU_LIMIT = 7.0
LN_EPS = 1e-5
NEG_BIG = -1e30
LOG2_E = 1.4426950408889634
EXP_ROWS = 32
DEN_ROWS = 16

TOKEN_TILE = 256
Q_TILE = 512
ATTN_HEADS = 2
SAMPLE_CHUNK = 2048
EXPERT_BLOCK = 512
DISPATCH_TILE = 1024
COMBINE_TILE = 256
DMA_UNROLL = 8
VMEM_LIMIT = 56 * 1024 * 1024

F_MID_SHIFT = N_HEADS
F_LO_SHIFT = 2 * N_HEADS
F_ONE_LANE = 3 * N_HEADS


def _dot(a, b):
    return jnp.dot(a, b, preferred_element_type=F32)


def _dot_nt(a, b):
    return lax.dot_general(a, b, (((1,), (1,)), ((), ())), preferred_element_type=F32)


def _layer_norm(x, g, b):
    mu = jnp.mean(x, axis=-1, keepdims=True)
    xc = x - mu
    var = jnp.mean(xc * xc, axis=-1, keepdims=True)
    return xc * lax.rsqrt(var + LN_EPS) * g + b


def _log_sigmoid(z):
    return jnp.minimum(z, 0.0) - jnp.log1p(jnp.exp(-jnp.abs(z)))


def _split3(x):
    hi = x.astype(BF16).astype(F32)
    r = x - hi
    mid = r.astype(BF16).astype(F32)
    lo = (r - mid).astype(BF16).astype(F32)
    return hi, mid, lo


def _tri(n, strict):
    r = lax.broadcasted_iota(I32, (n, n), 0)
    c = lax.broadcasted_iota(I32, (n, n), 1)
    keep = (r > c) if strict else (r >= c)
    return jnp.where(keep, 1.0, 0.0).astype(F32)


def _cumsum_rows(x):
    n = x.shape[0]
    hi, mid, lo = _split3(x)
    cat = jnp.concatenate([hi, mid, lo], axis=1).astype(BF16)
    c = _dot(_tri(n, strict=False).astype(BF16), cat)
    return (c[:, 2 * LANES:] + c[:, LANES:2 * LANES]) + c[:, :LANES]


def _proj_sections(x_ref, wuv, wkv, wf, wg, buv, bkv, bf_, bg, lng, lnb, u_o, g_o):
    xb = x_ref[...].astype(BF16)
    pu = _dot(xb, wuv[:, :D_MODEL]) + buv[:, :D_MODEL]
    u_o[...] = jax.nn.gelu(pu).astype(BF16)
    pv = _dot(xb, wuv[:, D_MODEL:]) + buv[:, D_MODEL:]
    va = _layer_norm(jax.nn.gelu(pv), lng[...], lnb[...])
    k = _dot(xb, wkv[:, :D_MODEL]) + bkv[:, :D_MODEL]
    v = _dot(xb, wkv[:, D_MODEL:]) + bkv[:, D_MODEL:]
    g_o[...] = jax.nn.sigmoid(_dot(xb, wg[...]) + bg[...]).astype(BF16)
    f = _dot(xb, wf[...]) + bf_[...]
    lane = lax.broadcasted_iota(I32, f.shape, 1)
    lf = jnp.where(lane < N_HEADS, _log_sigmoid(f), 0.0)
    return xb, va, k, v, lf


def _proj_prompt_kernel(x_ref, wuv, wq, wkv, wf, wg, buv, bq, bkv, bf_, bg, lng, lnb, eq, ek,
                        u_o, va_o, k_o, v_o, lf_o, g_o, qa_o, ka_o, vt_o, carry, *, tiles_per_seq):
    i = pl.program_id(0)
    xb, va, k, v, lf = _proj_sections(x_ref, wuv, wkv, wf, wg, buv, bkv, bf_, bg, lng, lnb, u_o, g_o)
    va_o[...] = va.astype(BF16)
    v_t = v.T
    k_o[0] = k.T.reshape(k_o.shape[1:])
    v_o[0] = v_t.reshape(v_o.shape[1:])
    vt_o[0, 0] = v_t.astype(BF16)
    lf_o[...] = lf[:, :N_HEADS]

    @pl.when(i % tiles_per_seq == 0)
    def _():
        carry[...] = jnp.zeros_like(carry)

    tm = lf.shape[0]
    fcum = carry[...] + _cumsum_rows(lf)
    carry[...] = fcum[tm - 1:tm, :]

    hi, mid, lo = _split3(fcum * LOG2_E)
    lane = lax.broadcasted_iota(I32, fcum.shape, 1)
    pack = (hi + pltpu.roll(mid, F_MID_SHIFT, 1) + pltpu.roll(lo, F_LO_SHIFT, 1)
            + jnp.where(lane == F_ONE_LANE, 1.0, 0.0)).astype(BF16)
    add_q = _dot(pack, eq[...])
    add_k = _dot(pack, ek[...])
    q = (_dot(xb, wq[...]) + bq[...]) * (HEAD_DIM ** -0.5 * LOG2_E)
    low =lane < HEAD_DIM
    for g in range(N_GROUPS):
        sl = slice(g * LANES, (g + 1) * LANES)
        ev = slice(2 * g * LANES, (2 * g + 1) * LANES)
        od = slice((2 * g + 1) * LANES, (2 * g + 2) * LANES)
        qa_o[:, ev] = jnp.where(low, q[:, sl], add_q[:, ev]).astype(BF16)
        qa_o[:, od] = jnp.where(low, add_q[:, od], q[:, sl]).astype(BF16)
        ka_o[:, ev] = jnp.where(low, k[:, sl], add_k[:, ev]).astype(BF16)
        ka_o[:, od] = jnp.where(low, add_k[:, od], k[:, sl]).astype(BF16)


def _proj_sample_kernel(x_ref, wuv, wq, wkv, wf, wg, buv, bq, bkv, bf_, bg, lng, lnb,
                        u_o, va_o, k_o, v_o, lf_o, g_o, q_o):
    xb, va, k, v, lf = _proj_sections(x_ref, wuv, wkv, wf, wg, buv, bkv, bf_, bg, lng, lnb, u_o, g_o)
    k_o[...] = k
    v_o[...] = v
    va_o[...] = va
    lf_o[...] = lf
    q_o[...] = (_dot(xb, wq[...]) + bq[...]) * (HEAD_DIM ** -0.5)


def _const_spec(shape):
    nd = len(shape)
    return pl.BlockSpec(shape, lambda *_: (0,) * nd, pipeline_mode=pl.Buffered(1))


def _proj_weights(w_in, b_in):
    d = D_MODEL
    off_q, off_k, off_f = 2 * d, 3 * d, 5 * d
    off_ga = off_f + N_HEADS
    wuv = w_in[:, :off_q].astype(BF16)
    wq = w_in[:, off_q:off_k].astype(BF16)
    wkv = w_in[:, off_k:off_f].astype(BF16)
    wf = jnp.pad(w_in[:, off_f:off_ga], ((0, 0), (0, LANES - N_HEADS))).astype(BF16)
    wg = w_in[:, off_ga:].astype(BF16)
    b = b_in.reshape(1, -1).astype(F32)
    buv, bq, bkv = b[:, :off_q], b[:, off_q:off_k], b[:, off_k:off_f]
    bf_ = jnp.pad(b[:, off_f:off_ga], ((0, 0), (0, LANES - N_HEADS)))
    bg = b[:, off_ga:]
    return (wuv, wq, wkv, wf, wg), (buv, bq, bkv, bf_, bg)


def _spread_matrices():
    eq_np = np.zeros((LANES, 2 * D_MODEL), np.float32)
    ek_np = np.zeros((LANES, 2 * D_MODEL), np.float32)
    for h in range(N_HEADS):
        base = LANES * h + (HEAD_DIM if h % 2 == 0 else 0)
        for c, shift in enumerate((0, F_MID_SHIFT, F_LO_SHIFT)):
            eq_np[shift + h, base + c] = 1.0
            eq_np[F_ONE_LANE, base + 3 + c] = 1.0
            ek_np[F_ONE_LANE, base + c] = 1.0
            ek_np[shift + h, base + 3 + c] = -1.0
    return jnp.asarray(eq_np, BF16), jnp.asarray(ek_np, BF16)


def _project_prompt(x2d, seq, ws, bs, lng, lnb):
    t = x2d.shape[0]
    tm = min(TOKEN_TILE, seq)
    nt = t // tm
    nb = t // seq
    eq, ek = _spread_matrices()
    consts = list(ws) + list(bs) + [lng, lnb, eq, ek]
    row = lambda w: pl.BlockSpec((tm, w), lambda i: (i, 0))
    out_shape = (
        jax.ShapeDtypeStruct((t, D_MODEL), BF16),
        jax.ShapeDtypeStruct((t, D_MODEL), BF16),
        jax.ShapeDtypeStruct((nb, N_HEADS, HEAD_DIM, seq), F32),
        jax.ShapeDtypeStruct((nb, N_HEADS, HEAD_DIM, seq), F32),
        jax.ShapeDtypeStruct((t, N_HEADS), F32),
        jax.ShapeDtypeStruct((t, 2 * D_MODEL), BF16),
        jax.ShapeDtypeStruct((t, 2 * D_MODEL), BF16),
        jax.ShapeDtypeStruct((t, 2 * D_MODEL), BF16),
        jax.ShapeDtypeStruct((nb, seq // tm, D_MODEL, tm), BF16),
    )
    tiles_per_seq = seq // tm
    heads = pl.BlockSpec((1, N_HEADS, HEAD_DIM, tm), lambda i: (i // tiles_per_seq, 0, 0, i % tiles_per_seq))
    out_specs = (row(D_MODEL), row(D_MODEL), heads, heads, row(N_HEADS),
                 row(2 * D_MODEL), row(2 * D_MODEL), row(2 * D_MODEL),
                 pl.BlockSpec((1, 1, D_MODEL, tm), lambda i: (i // tiles_per_seq, i % tiles_per_seq, 0, 0)))
    return pl.pallas_call(
        functools.partial(_proj_prompt_kernel, tiles_per_seq=tiles_per_seq),
        grid=(nt,),
        in_specs=[row(D_MODEL)] + [_const_spec(c.shape) for c in consts],
        out_specs=out_specs,
        out_shape=out_shape,
        scratch_shapes=[pltpu.VMEM((1, LANES), F32)],
        compiler_params=pltpu.CompilerParams(dimension_semantics=("arbitrary",), vmem_limit_bytes=VMEM_LIMIT),
        name="proj_prompt",
    )(x2d, *consts)


def _project_sample(x2d, ws, bs, lng, lnb):
    t = x2d.shape[0]
    tm = min(TOKEN_TILE, t)
    nt = t // tm
    consts = list(ws) + list(bs) + [lng, lnb]
    row = lambda w: pl.BlockSpec((tm, w), lambda i: (i, 0))
    out_shape = (
        jax.ShapeDtypeStruct((t, D_MODEL), BF16),
        jax.ShapeDtypeStruct((t, D_MODEL), F32),
        jax.ShapeDtypeStruct((t, D_MODEL), F32),
        jax.ShapeDtypeStruct((t, D_MODEL), F32),
        jax.ShapeDtypeStruct((t, LANES), F32),
        jax.ShapeDtypeStruct((t, 2 * D_MODEL), BF16),
        jax.ShapeDtypeStruct((t, D_MODEL), F32),
    )
    out_specs = (row(D_MODEL), row(D_MODEL), row(D_MODEL), row(D_MODEL), row(LANES), row(2 * D_MODEL), row(D_MODEL))
    return pl.pallas_call(
        _proj_sample_kernel,
        grid=(nt,),
        in_specs=[row(D_MODEL)] + [_const_spec(c.shape) for c in consts],
        out_specs=out_specs,
        out_shape=out_shape,
        compiler_params=pltpu.CompilerParams(dimension_semantics=("arbitrary",), vmem_limit_bytes=VMEM_LIMIT),
        name="proj_sample",
    )(x2d, *consts)


def _attn_prompt_kernel(q_ref, k_ref, vt_ref, o_ref, m_ref, acc_ref, ot_ref,
                        s0, s1, p0, p1, a0, a1, c0, c1, qt_sc, *, tq, tk):
    qi = pl.program_id(2)
    nh = m_ref.shape[0]
    assert tq == 2 * tk, "the pipeline below is written for two diagonal spans per query tile"
    m_ref[...] = jnp.full(m_ref.shape, NEG_BIG, F32)
    acc_ref[...] = jnp.zeros(acc_ref.shape, F32)
    p1[...] = jnp.zeros(p1.shape, BF16)
    a1[...] = jnp.ones(a1.shape, F32)
    tv = vt_ref.shape[3]
    ones = jnp.ones((DEN_ROWS, tv), BF16)
    lanes = lambda hh: slice(hh * LANES, (hh + 1) * LANES)

    def score(t, s_slot, c_slot, transposed_q=True):
        start = pl.multiple_of(t * tk, tk)
        for hh in range(nh):
            k = k_ref[pl.ds(start, tk), lanes(hh)]
            st = _dot(k, qt_sc[hh]) if transposed_q else _dot_nt(k, q_ref[:, lanes(hh)])
            s_slot[hh] = st
            c_slot[hh] = jnp.max(st, axis=0, keepdims=True)

    def softmax(s_slot, c_slot, p_slot, a_slot, mask_offset):
        for hh in range(nh):
            if mask_offset is None:
                col_max = c_slot[hh]
            else:
                key_row = lax.broadcasted_iota(I32, (tk, tq), 0)
                qry_col = lax.broadcasted_iota(I32, (tk, tq), 1)
                s_slot[hh] = jnp.where(key_row + mask_offset <= qry_col, s_slot[hh], NEG_BIG)
                col_max = jnp.max(s_slot[hh], axis=0, keepdims=True)
            m_old = m_ref[hh]
            m_new = jnp.maximum(m_old, col_max)
            a_slot[hh] = jnp.exp2(m_old - m_new)
            for r in range(0, tk, EXP_ROWS):
                rows = slice(r, r + EXP_ROWS)
                p_slot[hh, rows, :] = jnp.exp2((s_slot[hh, rows, :] - m_new).astype(BF16))
            m_ref[hh] = m_new

    def value(t, p_slot, a_slot):
        span = jnp.maximum(t, 0)
        for hh in range(nh):
            pv = None
            for d in range(tk // tv):
                vt = vt_ref[0, span * (tk // tv) + d, hh * HEAD_DIM:(hh + 1) * HEAD_DIM, :]
                part = _dot(jnp.concatenate([vt, ones], axis=0), p_slot[hh, d * tv:(d + 1) * tv, :])
                pv = part if pv is None else pv + part
            acc_ref[hh] = a_slot[hh] * acc_ref[hh] + pv

    score(0, s0, c0, transposed_q=False)
    for hh in range(nh):
        qt_sc[hh] = q_ref[:, lanes(hh)].astype(F32).T.astype(BF16)

    def two_steps(ii):
        t = 2 * ii + 1
        value(t - 2, p1, a1)
        score(t, s1, c1)
        softmax(s0, c0, p0, a0, None)
        value(t - 1, p0, a0)
        score(t + 1, s0, c0)
        softmax(s1, c1, p1, a1, None)

    def body(jj, c):
        two_steps(2 * jj)
        two_steps(2 * jj + 1)
        return c

    lax.fori_loop(0, qi // 2, body, 0)

    @pl.when(qi % 2 == 1)
    def _():
        two_steps(qi - 1)

    t = 2 * qi + 1
    value(t - 2, p1, a1)
    score(t, s1, c1)
    softmax(s0, c0, p0, a0, 0)
    value(t - 1, p0, a0)
    softmax(s1, c1, p1, a1, tk)
    value(t, p1, a1)
    for hh in range(nh):
        ot_ref[hh * HEAD_DIM:(hh + 1) * HEAD_DIM, :] = acc_ref[hh, :HEAD_DIM] / acc_ref[hh, HEAD_DIM:HEAD_DIM + 1]
    o_ref[...] = ot_ref[...].T.astype(BF16)


def _attend_prompt(q_aug, k_aug, v_t, nb, seq):
    t = q_aug.shape[0]
    tv = v_t.shape[3]
    tq = min(Q_TILE, seq)
    tk = tq // 2
    nq = seq // tq
    nh = ATTN_HEADS
    return pl.pallas_call(
        functools.partial(_attn_prompt_kernel, tq=tq, tk=tk),
        grid=(nb, N_HEADS // nh, nq),
        in_specs=[
            pl.BlockSpec((tq, nh * LANES), lambda b, g, i: (b * nq + i, g)),
            pl.BlockSpec((seq, nh * LANES), lambda b, g, i: (b, g)),
            pl.BlockSpec((1, seq // tv, nh * HEAD_DIM, tv), lambda b, g, i: (b, 0, g, 0)),
        ],
        out_specs=pl.BlockSpec((tq, nh * HEAD_DIM), lambda b, g, i: (b * nq + i, g)),
        out_shape=jax.ShapeDtypeStruct((t, D_MODEL), BF16),
        scratch_shapes=[pltpu.VMEM((nh, 1, tq), F32), pltpu.VMEM((nh, HEAD_DIM + DEN_ROWS, tq), F32),
                        pltpu.VMEM((nh * HEAD_DIM, tq), F32),
                        pltpu.VMEM((nh, tk, tq), F32), pltpu.VMEM((nh, tk, tq), F32),
                        pltpu.VMEM((nh, tk, tq), BF16), pltpu.VMEM((nh, tk, tq), BF16),
                        pltpu.VMEM((nh, 1, tq), F32), pltpu.VMEM((nh, 1, tq), F32),
                        pltpu.VMEM((nh, 1, tq), F32), pltpu.VMEM((nh, 1, tq), F32),
                        pltpu.VMEM((nh, LANES, tq), BF16)],
        compiler_params=pltpu.CompilerParams(
            dimension_semantics=("arbitrary", "arbitrary", "arbitrary"), vmem_limit_bytes=VMEM_LIMIT),
        name="attn_prompt",
    )(q_aug, k_aug, v_t)


def _attn_sample_kernel(q_ref, k_ref, v_ref, lf_ref, ck_ref, cv_ref, clf_ref, o_ref,
                        fct_sc, fn_sc, m_sc, l_sc, acc_sc, *, n_chunks):
    c = pl.program_id(1)
    chunk = ck_ref.shape[3]
    new = q_ref.shape[1]

    @pl.when(c == 0)
    def _():
        carry = jnp.zeros((1, LANES), F32)
        for cc in range(n_chunks):
            fc = carry + _cumsum_rows(clf_ref[0, cc * chunk:(cc + 1) * chunk, :])
            carry = fc[chunk - 1:chunk, :]
            fct_sc[cc] = fc.T
        fn_sc[...] = carry + _cumsum_rows(lf_ref[0])
        m_sc[...] = jnp.full(m_sc.shape, NEG_BIG, F32)
        l_sc[...] = jnp.zeros(l_sc.shape, F32)
        acc_sc[...] = jnp.zeros(acc_sc.shape, F32)

    lane = lax.broadcasted_iota(I32, (new, LANES), 1)
    f_new = fn_sc[...]

    def attend(score, weigh, f_keys_t, mask):
        for g in range(N_GROUPS):
            qg = q_ref[0, :, g * LANES:(g + 1) * LANES]
            for hh in range(2):
                h = 2 * g + hh
                mine = (lane < HEAD_DIM) if hh == 0 else (lane >= HEAD_DIM)
                qm = jnp.where(mine, qg, 0.0).astype(BF16)
                s = score(qm, g) + f_new[:, h:h + 1] - f_keys_t[h:h + 1, :]
                if mask is not None:
                    s = jnp.where(mask, s, NEG_BIG)
                m_old = m_sc[h]
                m_new = jnp.maximum(m_old, jnp.max(s, axis=1, keepdims=True))
                alpha = jnp.exp(m_old - m_new)
                p = jnp.exp(s - m_new)
                l_sc[h] = alpha * l_sc[h] + jnp.sum(p, axis=1, keepdims=True)
                acc_sc[h] = alpha * acc_sc[h] + weigh(p.astype(BF16), g)
                m_sc[h] = m_new

    pair_t = lambda ref, g: ref[0, 2 * g:2 * g + 2].reshape(LANES, chunk).astype(BF16)
    attend(lambda qm, g: _dot(qm, pair_t(ck_ref, g)), lambda p, g: _dot_nt(p, pair_t(cv_ref, g)), fct_sc[c], None)

    @pl.when(c == n_chunks - 1)
    def _():
        causal = lax.broadcasted_iota(I32, (new, new), 1) <= lax.broadcasted_iota(I32, (new, new), 0)
        cols = lambda ref, g: ref[0, :, g * LANES:(g + 1) * LANES].astype(BF16)
        attend(lambda qm, g: _dot_nt(qm, cols(k_ref, g)), lambda p, g: _dot(p, cols(v_ref, g)), f_new.T, causal)
        for g in range(N_GROUPS):
            even = acc_sc[2 * g] / l_sc[2 * g]
            odd = acc_sc[2 * g + 1] / l_sc[2 * g + 1]
            o_ref[:, g * LANES:(g + 1) * LANES] = jnp.where(lane < HEAD_DIM, even, odd).astype(BF16)


def _attend_sample(q, k, v, lf, cache_k, cache_v, cache_lf):
    nb, new, _ = q.shape
    past = cache_k.shape[3]
    chunk = min(SAMPLE_CHUNK, past)
    n_chunks = past // chunk
    blk = lambda n, w: pl.BlockSpec((1, n, w), lambda b, c: (b, 0, 0))
    cache = pl.BlockSpec((1, N_HEADS, HEAD_DIM, chunk), lambda b, c: (b, 0, 0, c))
    return pl.pallas_call(
        functools.partial(_attn_sample_kernel, n_chunks=n_chunks),
        grid=(nb, n_chunks),
        in_specs=[blk(new, D_MODEL), blk(new, D_MODEL), blk(new, D_MODEL), blk(new, LANES),
                  cache, cache, blk(past, LANES)],
        out_specs=pl.BlockSpec((new, D_MODEL), lambda b, c: (b, 0)),
        out_shape=jax.ShapeDtypeStruct((nb * new, D_MODEL), BF16),
        scratch_shapes=[pltpu.VMEM((n_chunks, LANES, chunk), F32), pltpu.VMEM((new, LANES), F32),
                        pltpu.VMEM((N_HEADS, new, 1), F32), pltpu.VMEM((N_HEADS, new, 1), F32),
                        pltpu.VMEM((N_HEADS, new, LANES), F32)],
        compiler_params=pltpu.CompilerParams(dimension_semantics=("arbitrary", "arbitrary"),
                                             vmem_limit_bytes=VMEM_LIMIT),
        name="attn_sample",
    )(q, k, v, lf, cache_k, cache_v, cache_lf)


def _post_kernel(x_ref, u_ref, va_ref, g_ref, at_ref, p_ref, base_ref, ws_ref, bst_ref, wpa, wpb, wo,
                 l1g, l1b, wr_both, br, wpg, bpg, wpp,
                 x1_o, resid_o, topi_o, gate_o, rank_o, cnt_o, a_sc, base_sc, *, chunk, dn_alpha):
    i = pl.program_id(0)
    tm = x_ref.shape[0]

    @pl.when(i == 0)
    def _():
        base_sc[...] = base_ref[...]

    tri = _tri(chunk, strict=False)
    zeros = jnp.zeros((chunk, LANES), BF16)
    lane2 = lax.broadcasted_iota(I32, (chunk, 2 * LANES), 1)
    for g in range(0, N_GROUPS, 2):
        sl = slice(g * LANES, (g + 2) * LANES)
        wm = jnp.concatenate([ws_ref[g, :chunk, :chunk] * tri, ws_ref[g + 1, :chunk, :chunk] * tri], axis=1).astype(BF16)
        bias = jnp.where(lane2 < LANES, bst_ref[:chunk, g:g + 1], bst_ref[:chunk, g + 1:g + 2])
        for c in range(tm // chunk):
            rows = slice(c * chunk, (c + 1) * chunk)
            va = va_ref[rows, sl]
            diag = jnp.concatenate([jnp.concatenate([va[:, :LANES], zeros], axis=1),
                                    jnp.concatenate([zeros, va[:, LANES:]], axis=1)], axis=0)
            sg = _dot(wm, diag) + bias
            a_sc[rows, sl] = (u_ref[rows, sl].astype(F32) * sg).astype(BF16)

    pa = _dot(a_sc[...], wpa[...])
    pb = _dot(at_ref[...], wpb[...])
    merged = g_ref[:, :D_MODEL].astype(F32) * pa + g_ref[:, D_MODEL:].astype(F32) * pb
    x1 = _layer_norm(dn_alpha * x_ref[...] + _dot(merged.astype(BF16), wo[...]), l1g[...], l1b[...])
    x1b = x1.astype(BF16)
    x1_o[...] = x1

    ple = jax.nn.sigmoid(_dot(x1b, wpg[...]) + bpg[...]) * _dot(p_ref[...].astype(BF16), wpp[...])
    resid_o[...] = dn_alpha * x1 + ple

    x1_lo = (x1 - x1b.astype(F32)).astype(BF16)
    both = _dot(x1b, wr_both[...])
    logits = (both[:, :LANES] + (both[:, LANES:] + _dot(x1_lo, wr_both[:, :LANES]))) + br[...]
    lane = lax.broadcasted_iota(I32, (tm, LANES), 1)
    work = logits
    vals, idxs = [], []
    for _ in range(TOP_K):
        mk = jnp.max(work, axis=1, keepdims=True)
        ik = jnp.min(jnp.where(work == mk, lane, LANES), axis=1, keepdims=True)
        vals.append(mk)
        idxs.append(ik)
        work = jnp.where(lane == ik, -jnp.inf, work)
    exps = [jnp.exp(v - vals[0]) for v in vals]
    den = exps[0] + exps[1] + exps[2] + exps[3]

    lower = _tri(tm, strict=True).astype(BF16)
    run = base_sc[...]
    topi = jnp.zeros((tm, LANES), I32)
    gate = jnp.zeros((tm, LANES), F32)
    rank = jnp.zeros((tm, LANES), F32)
    for k in range(TOP_K):
        onehot = jnp.where(lane == idxs[k], 1.0, 0.0)
        before = _dot(lower, onehot.astype(BF16))
        rk = jnp.sum(onehot * (before + run), axis=1, keepdims=True)
        topi = jnp.where(lane == k, idxs[k], topi)
        gate = jnp.where(lane == k, exps[k] / den, gate)
        rank = jnp.where(lane == k, rk, rank)
        run = run + jnp.sum(onehot, axis=0, keepdims=True)
    base_sc[...] = run
    topi_o[...] = topi
    gate_o[...] = gate
    rank_o[...] = rank.astype(I32)
    cnt_o[...] = run


def _post_attention(x2d, u, va, gates, attn, p2d, base, consts, chunk, dn_alpha):
    t = x2d.shape[0]
    tm = min(TOKEN_TILE, t)
    nt = t // tm
    row = lambda w: pl.BlockSpec((tm, w), lambda i: (i, 0))
    ple_dim = p2d.shape[1]
    out_shape = (
        jax.ShapeDtypeStruct((t, D_MODEL), F32),
        jax.ShapeDtypeStruct((t, D_MODEL), F32),
        jax.ShapeDtypeStruct((t, LANES), I32),
        jax.ShapeDtypeStruct((t, LANES), F32),
        jax.ShapeDtypeStruct((t, LANES), I32),
        jax.ShapeDtypeStruct((1, LANES), F32),
    )
    out_specs = (row(D_MODEL), row(D_MODEL), row(LANES), row(LANES), row(LANES),
                 pl.BlockSpec((1, LANES), lambda i: (0, 0)))
    return pl.pallas_call(
        functools.partial(_post_kernel, chunk=chunk, dn_alpha=dn_alpha),
        grid=(nt,),
        in_specs=[row(D_MODEL), row(D_MODEL), row(D_MODEL), row(2 * D_MODEL), row(D_MODEL), row(ple_dim),
                  _const_spec(base.shape)] + [_const_spec(c.shape) for c in consts],
        out_specs=out_specs,
        out_shape=out_shape,
        scratch_shapes=[pltpu.VMEM((tm, D_MODEL), BF16), pltpu.VMEM((1, LANES), F32)],
        compiler_params=pltpu.CompilerParams(dimension_semantics=("arbitrary",), vmem_limit_bytes=VMEM_LIMIT),
        name="post_attention",
    )(x2d, u, va, gates, attn, p2d, base, *consts)


def _dispatch_kernel(fill_start_ref, fill_len_ref, nu_ref, xp_ref, xn_ref, dest_ref, xs_out, zero_sc, sem, zsem,
                     *, prompt_steps, n_blocks):
    i = pl.program_id(0)
    td = xp_ref.shape[0]

    @pl.when(i == 0)
    def _():
        zero_sc[...] = jnp.zeros(zero_sc.shape, zero_sc.dtype)

        def zero_fill(wait):
            def go(copy):
                copy.wait() if wait else copy.start()

            def per_expert(e, c):
                n = fill_len_ref[e]
                base = fill_start_ref[e]
                head = jnp.bitwise_and(n, SUBLANES - 1)
                for r in range(SUBLANES - 1):
                    @pl.when(r < head)
                    def _(r=r):
                        go(pltpu.make_async_copy(zero_sc.at[pl.ds(0, 1)], xs_out.at[pl.ds(base + r, 1)], zsem))

                piece = EXPERT_BLOCK // 2
                while piece >= SUBLANES:
                    offset = pl.multiple_of(base + head + jnp.bitwise_and(n, -(2 * piece)), SUBLANES)

                    @pl.when(jnp.bitwise_and(n, piece) != 0)
                    def _(piece=piece, offset=offset):
                        go(pltpu.make_async_copy(zero_sc.at[pl.ds(0, piece)], xs_out.at[pl.ds(offset, piece)], zsem))

                    piece //= 2
                return c

            lax.fori_loop(0, N_EXPERTS, per_expert, 0)

            def per_block(j, c):
                row0 = pl.multiple_of(j * EXPERT_BLOCK, EXPERT_BLOCK)
                go(pltpu.make_async_copy(zero_sc, xs_out.at[pl.ds(row0, EXPERT_BLOCK)], zsem))
                return c

            lax.fori_loop(nu_ref[0], n_blocks, per_block, 0)

        zero_fill(wait=False)
        zero_fill(wait=True)

    def scatter(x_ref):
        def body(j, c):
            for u in range(DMA_UNROLL):
                tok = j * (DMA_UNROLL // TOP_K) + u // TOP_K
                pltpu.make_async_copy(x_ref.at[pl.ds(tok, 1)], xs_out.at[pl.ds(dest_ref[j * DMA_UNROLL + u], 1)],
                                      sem).start(priority=u % 2)
            return c

        lax.fori_loop(0, td * TOP_K // DMA_UNROLL, body, 0)
        for _ in range(TOP_K):
            pltpu.make_async_copy(x_ref, xs_out.at[pl.ds(0, td)], sem).wait()

    @pl.when(i < prompt_steps)
    def _():
        scatter(xp_ref)

    @pl.when(i >= prompt_steps)
    def _():
        scatter(xn_ref)


def _dispatch(fill_start, fill_len, n_used, x1_prompt, x1_sample, dest, n_rows):
    tp, w = x1_prompt.shape
    ts = x1_sample.shape[0]
    td = min(DISPATCH_TILE, ts)
    prompt_steps, sample_steps = tp // td, ts // td
    grid_spec = pltpu.PrefetchScalarGridSpec(
        num_scalar_prefetch=3,
        grid=(prompt_steps + sample_steps,),
        in_specs=[pl.BlockSpec((td, w), lambda i, *_: (jnp.minimum(i, prompt_steps - 1), 0)),
                  pl.BlockSpec((td, w), lambda i, *_: (jnp.maximum(i - prompt_steps, 0), 0)),
                  pl.BlockSpec((td * TOP_K,), lambda i, *_: (i,), memory_space=pltpu.SMEM)],
        out_specs=pl.BlockSpec(memory_space=pl.ANY),
        scratch_shapes=[pltpu.VMEM((EXPERT_BLOCK, w), x1_prompt.dtype),
                        pltpu.SemaphoreType.DMA(()), pltpu.SemaphoreType.DMA(())],
    )
    return pl.pallas_call(
        functools.partial(_dispatch_kernel, prompt_steps=prompt_steps, n_blocks=n_rows // EXPERT_BLOCK),
        grid_spec=grid_spec,
        out_shape=jax.ShapeDtypeStruct((n_rows, w), x1_prompt.dtype),
        compiler_params=pltpu.CompilerParams(dimension_semantics=("arbitrary",), vmem_limit_bytes=VMEM_LIMIT),
        name="dispatch",
    )(fill_start, fill_len, n_used, x1_prompt, x1_sample, dest)


def _expert_kernel(be_ref, nu_ref, next_ref, slot_ref, xs_ref, wgu, bgu, wdn, bdn, ys_o,
                   wgu_in, wdn_in, wgu_bf, wdn_bf, sem):
    j = pl.program_id(0)
    used = j < nu_ref[0]
    new_expert = jnp.logical_or(j == 0, be_ref[j] != be_ref[jnp.maximum(j - 1, 0)])
    slot = slot_ref[j]

    def weight_copies(e, s):
        return (pltpu.make_async_copy(wgu.at[e], wgu_in.at[s], sem.at[0, s]),
                pltpu.make_async_copy(wdn.at[e], wdn_in.at[s], sem.at[1, s]))

    @pl.when(j == 0)
    def _():
        for cp in weight_copies(be_ref[0], slot):
            cp.start()

    @pl.when(jnp.logical_not(used))
    def _():
        ys_o[...] = jnp.zeros(ys_o.shape, ys_o.dtype)

    @pl.when(jnp.logical_and(used, new_expert))
    def _():
        for cp in weight_copies(be_ref[j], slot):
            cp.wait()
        wgu_bf[...] = wgu_in[slot].astype(BF16)
        wdn_bf[...] = wdn_in[slot].astype(BF16)
        nxt = next_ref[j]

        @pl.when(nxt >= 0)
        def _():
            for cp in weight_copies(nxt, 1 - slot):
                cp.start()

    @pl.when(used)
    def _():
        h = _dot(xs_ref[...].astype(BF16), wgu_bf[...]) + bgu[0]
        d_e = h.shape[1] // 2
        gate = jnp.minimum(h[:, :d_e], SWIGLU_LIMIT)
        up = jnp.clip(h[:, d_e:], -SWIGLU_LIMIT, SWIGLU_LIMIT)
        glu = gate * jax.nn.sigmoid(gate * SWIGLU_ALPHA)
        ys_o[...] = _dot(((up + 1.0) * glu).astype(BF16), wdn_bf[...]) + bdn[0]


def _experts(block_expert, n_used, next_expert, block_slot, xs, wgu, bgu, wdn, bdn):
    rows, w = xs.shape
    nblk = rows // EXPERT_BLOCK
    d_in, d_gu = wgu.shape[1], wgu.shape[2]
    d_e, d_out = wdn.shape[1], wdn.shape[2]
    grid_spec = pltpu.PrefetchScalarGridSpec(
        num_scalar_prefetch=4,
        grid=(nblk,),
        in_specs=[
            pl.BlockSpec((EXPERT_BLOCK, w), lambda j, *_: (j, 0)),
            pl.BlockSpec(memory_space=pl.ANY),
            pl.BlockSpec((1, 1, d_gu), lambda j, be, *_: (be[j], 0, 0)),
            pl.BlockSpec(memory_space=pl.ANY),
            pl.BlockSpec((1, 1, d_out), lambda j, be, *_: (be[j], 0, 0)),
        ],
        out_specs=pl.BlockSpec((EXPERT_BLOCK, d_out), lambda j, *_: (j, 0)),
        scratch_shapes=[pltpu.VMEM((2, d_in, d_gu), F32), pltpu.VMEM((2, d_e, d_out), F32),
                        pltpu.VMEM((d_in, d_gu), BF16), pltpu.VMEM((d_e, d_out), BF16),
                        pltpu.SemaphoreType.DMA((2, 2))],
    )
    return pl.pallas_call(
        _expert_kernel,
        grid_spec=grid_spec,
        out_shape=jax.ShapeDtypeStruct((rows, d_out), F32),
        compiler_params=pltpu.CompilerParams(dimension_semantics=("arbitrary",), vmem_limit_bytes=VMEM_LIMIT),
        name="experts",
    )(block_expert, n_used, next_expert, block_slot, xs, wgu, bgu, wdn, bdn)


def _final_kernel(resid_ref, gate_ref, dest_ref, dest_next_ref, ys_ref, lng, lnb, y_o, buf, sem, *, n):
    i = pl.program_id(0)
    tg = resid_ref.shape[0]

    def gather(idx_ref, slot):
        def body(j, c):
            for u in range(DMA_UNROLL):
                tok = j * (DMA_UNROLL // TOP_K) + u // TOP_K
                pltpu.make_async_copy(ys_ref.at[pl.ds(idx_ref[j * DMA_UNROLL + u], 1)],
                                      buf.at[slot, u % TOP_K, pl.ds(tok, 1)], sem.at[slot]).start(priority=u % 2)
            return c

        lax.fori_loop(0, tg * TOP_K // DMA_UNROLL, body, 0)

    @pl.when(i == 0)
    def _():
        gather(dest_ref, 0)

    @pl.when(i + 1 < n)
    def _():
        gather(dest_next_ref, (i + 1) % 2)

    slot = i % 2
    for k in range(TOP_K):
        pltpu.make_async_copy(ys_ref.at[pl.ds(0, tg)], buf.at[slot, k], sem.at[slot]).wait()
    acc = resid_ref[...]
    for k in range(TOP_K):
        acc = acc + gate_ref[:, k:k + 1] * buf[slot, k]
    y_o[...] = _layer_norm(acc, lng[...], lnb[...])


def _combine(resid, gate, dest, ys, lng, lnb):
    t = resid.shape[0]
    tg = min(COMBINE_TILE, t)
    nt = t // tg
    row = lambda w: pl.BlockSpec((tg, w), lambda i: (i, 0))
    return pl.pallas_call(
        functools.partial(_final_kernel, n=nt),
        grid=(nt,),
        in_specs=[row(D_MODEL), row(LANES),
                  pl.BlockSpec((tg * TOP_K,), lambda i: (i,), memory_space=pltpu.SMEM),
                  pl.BlockSpec((tg * TOP_K,), lambda i: (jnp.minimum(i + 1, nt - 1),), memory_space=pltpu.SMEM),
                  pl.BlockSpec(memory_space=pl.ANY),
                  _const_spec(lng.shape), _const_spec(lnb.shape)],
        out_specs=row(D_MODEL),
        out_shape=jax.ShapeDtypeStruct((t, D_MODEL), F32),
        scratch_shapes=[pltpu.VMEM((2, TOP_K, tg, ys.shape[1]), F32), pltpu.SemaphoreType.DMA((2,))],
        compiler_params=pltpu.CompilerParams(dimension_semantics=("arbitrary",), vmem_limit_bytes=VMEM_LIMIT),
        name="combine",
    )(resid, gate, dest, dest, ys, lng, lnb)


def _layer(x_prompt, x_sample, cache_k, cache_v, cache_logf, p_prompt, p_sample,
           w_in, b_in, lnv_g, lnv_b, w_s, b_s, w_pa, w_pb, w_o, ln1_g, ln1_b,
           w_router, b_router, w_gu, b_gu, w_dn, b_dn, w_ple_gate, b_ple_gate, w_ple_proj, ln2_g, ln2_b,
           dn_alpha):
    nb, seq, d = x_prompt.shape
    sb, new, _ = x_sample.shape
    past = cache_k.shape[1]
    tp, ts = nb * seq, sb * new
    vec = lambda a: a.reshape(1, -1).astype(F32)

    ws, bs = _proj_weights(w_in, b_in)
    lng, lnb = vec(lnv_g), vec(lnv_b)
    xp = x_prompt.reshape(tp, d)
    xsm = x_sample.reshape(ts, d)
    u_p, va_p, k_p, v_p, lf_p, g_p, qa_p, ka_p, vt_p = _project_prompt(xp, seq, ws, bs, lng, lnb)
    u_s, va_s, k_s, v_s, lf_s, g_s, q_s = _project_sample(xsm, ws, bs, lng, lnb)

    attn_p = _attend_prompt(qa_p, ka_p, vt_p, nb, seq)
    cache_lf = jnp.pad(cache_logf.astype(F32), ((0, 0), (0, 0), (0, LANES - N_HEADS)))
    rows_last = lambda a: jnp.transpose(a, (0, 2, 3, 1))
    attn_s = _attend_sample(q_s.reshape(sb, new, d), k_s.reshape(sb, new, d), v_s.reshape(sb, new, d),
                            lf_s.reshape(sb, new, LANES), rows_last(cache_k), rows_last(cache_v), cache_lf)

    wr = jnp.pad(w_router.astype(F32), ((0, 0), (0, LANES - N_EXPERTS)))
    wr_hi = wr.astype(BF16)
    wr_lo = (wr - wr_hi.astype(F32)).astype(BF16)
    br = jnp.concatenate([b_router.astype(F32), jnp.full((LANES - N_EXPERTS,), NEG_BIG, F32)]).reshape(1, LANES)
    spatial = b_s.shape[1]
    bst = jnp.pad(b_s.astype(F32).T, ((0, 0), (0, LANES - b_s.shape[0])))
    post_consts = [w_s.astype(F32), bst, w_pa.astype(BF16), w_pb.astype(BF16), w_o.astype(BF16),
                   vec(ln1_g), vec(ln1_b), jnp.concatenate([wr_hi, wr_lo], axis=1), br,
                   w_ple_gate.astype(BF16), vec(b_ple_gate), w_ple_proj.astype(BF16)]
    base0 = jnp.zeros((1, LANES), F32)
    x1r_p, resid_p, topi_p, gate_p, rank_p, cnt_p = _post_attention(
        xp, u_p, va_p, g_p, attn_p, p_prompt.reshape(tp, -1), base0, post_consts, min(spatial, seq), dn_alpha)
    x1r_s, resid_s, topi_s, gate_s, rank_s, cnt = _post_attention(
        xsm, u_s, va_s.astype(BF16), g_s, attn_s, p_sample.reshape(ts, -1), cnt_p, post_consts,
        min(spatial, new), dn_alpha)

    counts = cnt[0, :N_EXPERTS].astype(I32)
    padded = (counts + EXPERT_BLOCK - 1) // EXPERT_BLOCK * EXPERT_BLOCK
    pad_end = jnp.cumsum(padded)
    pad_start = (pad_end - padded).astype(I32)
    n_assign = (tp + ts) * TOP_K
    nblk = -(-n_assign // EXPERT_BLOCK) + N_EXPERTS
    n_rows = nblk * EXPERT_BLOCK
    block_row0 = jnp.arange(nblk, dtype=I32) * EXPERT_BLOCK
    block_expert = jnp.minimum(jnp.sum(pad_end[None, :] <= block_row0[:, None], axis=1), N_EXPERTS - 1).astype(I32)
    n_used = (pad_end[-1:] // EXPERT_BLOCK).astype(I32)

    dest_of = lambda topi, rank: (jnp.take(pad_start, topi[:, :TOP_K]) + rank[:, :TOP_K]).reshape(-1)
    dest_p, dest_s = dest_of(topi_p, rank_p), dest_of(topi_s, rank_s)
    xs = _dispatch(pad_start + counts, padded - counts, n_used, x1r_p, x1r_s,
                   jnp.concatenate([dest_p, dest_s]), n_rows)

    expert_ids = jnp.arange(N_EXPERTS, dtype=I32)
    present = padded > 0
    later = jnp.where(jnp.logical_and(present[None, :], expert_ids[None, :] > expert_ids[:, None]),
                      expert_ids[None, :], N_EXPERTS)
    next_present = jnp.min(later, axis=1)
    next_present = jnp.where(next_present == N_EXPERTS, -1, next_present).astype(I32)
    order = (jnp.cumsum(present.astype(I32)) - 1).astype(I32)
    ys = _experts(block_expert, n_used, jnp.take(next_present, block_expert), jnp.take(order, block_expert) % 2,
                  xs, w_gu.astype(F32), b_gu.astype(F32)[:, None, :], w_dn.astype(F32), b_dn.astype(F32)[:, None, :])

    l2g, l2b = vec(ln2_g), vec(ln2_b)
    y_p = _combine(resid_p, gate_p, dest_p, ys, l2g, l2b)
    y_s = _combine(resid_s, gate_s, dest_s, ys, l2g, l2b)

    shp = lambda a, b_, n: a.reshape(b_, n, N_HEADS, HEAD_DIM)
    rows_first = lambda a: jnp.transpose(a, (0, 3, 1, 2))
    return (y_p.reshape(nb, seq, d), y_s.reshape(sb, new, d),
            rows_first(k_p), rows_first(v_p), lf_p.reshape(nb, seq, N_HEADS),
            shp(k_s, sb, new), shp(v_s, sb, new), lf_s[:, :N_HEADS].reshape(sb, new, N_HEADS),
            va_s.reshape(sb, new, N_GROUPS, d // N_GROUPS))


def kernel(x_prompt, x_sample, cache_fox_k, cache_fox_v, cache_fox_logf, p_prompt, p_sample, w_in, b_in, lnv_g, lnv_b, w_s, b_s, w_pa, w_pb, w_o, ln1_g, ln1_b, w_router, b_router, w_gu, b_gu, w_dn, b_dn, w_ple_gate, b_ple_gate, w_ple_proj, ln2_g, ln2_b):
    depth = w_in.shape[0]
    assert depth == 1, "the layer loop below carries one layer"
    dn_alpha = float((2 * depth) ** 0.25)
    outs = _layer(x_prompt, x_sample, cache_fox_k[0], cache_fox_v[0], cache_fox_logf[0], p_prompt[0], p_sample[0],
                  w_in[0], b_in[0], lnv_g[0], lnv_b[0], w_s[0], b_s[0], w_pa[0], w_pb[0], w_o[0], ln1_g[0], ln1_b[0],
                  w_router[0], b_router[0], w_gu[0], b_gu[0], w_dn[0], b_dn[0], w_ple_gate[0], b_ple_gate[0],
                  w_ple_proj[0], ln2_g[0], ln2_b[0], dn_alpha)
    y_p, y_s, k_p, v_p, lf_p, k_s, v_s, lf_s, va_s = outs
    lead = lambda a: a[None]
    return (y_p, y_s, lead(k_p), lead(v_p), lead(lf_p), lead(k_s), lead(v_s), lead(lf_s), lead(va_s))
```

```python
import functools

import jax
import jax.numpy as jnp
import numpy as np
from jax import lax
from jax.experimental import pallas as pl
from jax.experimental.pallas import tpu as pltpu

F32 = jnp.float32
BF16 = jnp.bfloat16
I32 = jnp.int32

LANES = 128
SUBLANES = 8
D_MODEL = 1024
N_HEADS = 16
HEAD_DIM = 64
N_GROUPS = 8
N_EXPERTS = 32
TOP_K = 4
SWIGLU_ALPHA = 1.702
SWIGLU_LIMIT = 7.0
LN_EPS = 1e-5
NEG_BIG = -1e30
LOG2_E = 1.4426950408889634
EXP_ROWS = 32
DEN_ROWS = 16

TOKEN_TILE = 256
Q_TILE = 512
ATTN_HEADS = 2
SAMPLE_CHUNK = 2048
EXPERT_BLOCK = 512
DISPATCH_TILE = 1024
COMBINE_TILE = 256
DMA_UNROLL = 32
VMEM_LIMIT = 56 * 1024 * 1024

F_MID_SHIFT = N_HEADS
F_LO_SHIFT = 2 * N_HEADS
F_ONE_LANE = 3 * N_HEADS


def _dot(a, b):
    return jnp.dot(a, b, preferred_element_type=F32)


def _dot_nt(a, b):
    return lax.dot_general(a, b, (((1,), (1,)), ((), ())), preferred_element_type=F32)


def _layer_norm(x, g, b):
    mu = jnp.mean(x, axis=-1, keepdims=True)
    xc = x - mu
    var = jnp.mean(xc * xc, axis=-1, keepdims=True)
    return xc * lax.rsqrt(var + LN_EPS) * g + b


def _log_sigmoid(z):
    return jnp.minimum(z, 0.0) - jnp.log1p(jnp.exp(-jnp.abs(z)))


def _split3(x):
    hi = x.astype(BF16).astype(F32)
    r = x - hi
    mid = r.astype(BF16).astype(F32)
    lo = (r - mid).astype(BF16).astype(F32)
    return hi, mid, lo


def _tri(n, strict):
    r = lax.broadcasted_iota(I32, (n, n), 0)
    c = lax.broadcasted_iota(I32, (n, n), 1)
    keep = (r > c) if strict else (r >= c)
    return jnp.where(keep, 1.0, 0.0).astype(F32)


def _cumsum_rows(x):
    n = x.shape[0]
    hi, mid, lo = _split3(x)
    cat = jnp.concatenate([hi, mid, lo], axis=1).astype(BF16)
    c = _dot(_tri(n, strict=False).astype(BF16), cat)
    return (c[:, 2 * LANES:] + c[:, LANES:2 * LANES]) + c[:, :LANES]


def _proj_sections(x_ref, wuv, wkv, wf, wg, buv, bkv, bf_, bg, lng, lnb, u_o, g_o):
    xb = x_ref[...].astype(BF16)
    pu = _dot(xb, wuv[:, :D_MODEL]) + buv[:, :D_MODEL]
    u_o[...] = jax.nn.gelu(pu).astype(BF16)
    pv = _dot(xb, wuv[:, D_MODEL:]) + buv[:, D_MODEL:]
    va = _layer_norm(jax.nn.gelu(pv), lng[...], lnb[...])
    k = _dot(xb, wkv[:, :D_MODEL]) + bkv[:, :D_MODEL]
    v = _dot(xb, wkv[:, D_MODEL:]) + bkv[:, D_MODEL:]
    g_o[...] = jax.nn.sigmoid(_dot(xb, wg[...]) + bg[...]).astype(BF16)
    f = _dot(xb, wf[...]) + bf_[...]
    lane = lax.broadcasted_iota(I32, f.shape, 1)
    lf = jnp.where(lane < N_HEADS, _log_sigmoid(f), 0.0)
    return xb, va, k, v, lf


def _proj_prompt_kernel(x_ref, wuv, wq, wkv, wf, wg, buv, bq, bkv, bf_, bg, lng, lnb, eq, ek,
                        u_o, va_o, k_o, v_o, lf_o, g_o, qa_o, ka_o, vt_o, carry, *, tiles_per_seq):
    i = pl.program_id(0)
    xb, va, k, v, lf = _proj_sections(x_ref, wuv, wkv, wf, wg, buv, bkv, bf_, bg, lng, lnb, u_o, g_o)
    va_o[...] = va.astype(BF16)
    v_t = v.T
    k_o[0] = k.T.reshape(k_o.shape[1:])
    v_o[0] = v_t.reshape(v_o.shape[1:])
    vt_o[0, 0] = v_t.astype(BF16)
    lf_o[...] = lf[:, :N_HEADS]

    @pl.when(i % tiles_per_seq == 0)
    def _():
        carry[...] = jnp.zeros_like(carry)

    tm = lf.shape[0]
    fcum = carry[...] + _cumsum_rows(lf)
    carry[...] = fcum[tm - 1:tm, :]

    hi, mid, lo = _split3(fcum * LOG2_E)
    lane = lax.broadcasted_iota(I32, fcum.shape, 1)
    pack = (hi + pltpu.roll(mid, F_MID_SHIFT, 1) + pltpu.roll(lo, F_LO_SHIFT, 1)
            + jnp.where(lane == F_ONE_LANE, 1.0, 0.0)).astype(BF16)
    add_q = _dot(pack, eq[...])
    add_k = _dot(pack, ek[...])
    q = (_dot(xb, wq[...]) + bq[...]) * (HEAD_DIM ** -0.5 * LOG2_E)
    low =lane < HEAD_DIM
    for g in range(N_GROUPS):
        sl = slice(g * LANES, (g + 1) * LANES)
        ev = slice(2 * g * LANES, (2 * g + 1) * LANES)
        od = slice((2 * g + 1) * LANES, (2 * g + 2) * LANES)
        qa_o[:, ev] = jnp.where(low, q[:, sl], add_q[:, ev]).astype(BF16)
        qa_o[:, od] = jnp.where(low, add_q[:, od], q[:, sl]).astype(BF16)
        ka_o[:, ev] = jnp.where(low, k[:, sl], add_k[:, ev]).astype(BF16)
        ka_o[:, od] = jnp.where(low, add_k[:, od], k[:, sl]).astype(BF16)


def _proj_sample_kernel(x_ref, wuv, wq, wkv, wf, wg, buv, bq, bkv, bf_, bg, lng, lnb,
                        u_o, va_o, k_o, v_o, lf_o, g_o, q_o):
    xb, va, k, v, lf = _proj_sections(x_ref, wuv, wkv, wf, wg, buv, bkv, bf_, bg, lng, lnb, u_o, g_o)
    k_o[...] = k
    v_o[...] = v
    va_o[...] = va
    lf_o[...] = lf
    q_o[...] = (_dot(xb, wq[...]) + bq[...]) * (HEAD_DIM ** -0.5)


def _const_spec(shape):
    nd = len(shape)
    return pl.BlockSpec(shape, lambda *_: (0,) * nd, pipeline_mode=pl.Buffered(1))


def _proj_weights(w_in, b_in):
    d = D_MODEL
    off_q, off_k, off_f = 2 * d, 3 * d, 5 * d
    off_ga = off_f + N_HEADS
    wuv = w_in[:, :off_q].astype(BF16)
    wq = w_in[:, off_q:off_k].astype(BF16)
    wkv = w_in[:, off_k:off_f].astype(BF16)
    wf = jnp.pad(w_in[:, off_f:off_ga], ((0, 0), (0, LANES - N_HEADS))).astype(BF16)
    wg = w_in[:, off_ga:].astype(BF16)
    b = b_in.reshape(1, -1).astype(F32)
    buv, bq, bkv = b[:, :off_q], b[:, off_q:off_k], b[:, off_k:off_f]
    bf_ = jnp.pad(b[:, off_f:off_ga], ((0, 0), (0, LANES - N_HEADS)))
    bg = b[:, off_ga:]
    return (wuv, wq, wkv, wf, wg), (buv, bq, bkv, bf_, bg)


def _spread_matrices():
    eq_np = np.zeros((LANES, 2 * D_MODEL), np.float32)
    ek_np = np.zeros((LANES, 2 * D_MODEL), np.float32)
    for h in range(N_HEADS):
        base = LANES * h + (HEAD_DIM if h % 2 == 0 else 0)
        for c, shift in enumerate((0, F_MID_SHIFT, F_LO_SHIFT)):
            eq_np[shift + h, base + c] = 1.0
            eq_np[F_ONE_LANE, base + 3 + c] = 1.0
            ek_np[F_ONE_LANE, base + c] = 1.0
            ek_np[shift + h, base + 3 + c] = -1.0
    return jnp.asarray(eq_np, BF16), jnp.asarray(ek_np, BF16)


def _project_prompt(x2d, seq, ws, bs, lng, lnb):
    t = x2d.shape[0]
    tm = min(TOKEN_TILE, seq)
    nt = t // tm
    nb = t // seq
    eq, ek = _spread_matrices()
    consts = list(ws) + list(bs) + [lng, lnb, eq, ek]
    row = lambda w: pl.BlockSpec((tm, w), lambda i: (i, 0))
    out_shape = (
        jax.ShapeDtypeStruct((t, D_MODEL), BF16),
        jax.ShapeDtypeStruct((t, D_MODEL), BF16),
        jax.ShapeDtypeStruct((nb, N_HEADS, HEAD_DIM, seq), F32),
        jax.ShapeDtypeStruct((nb, N_HEADS, HEAD_DIM, seq), F32),
        jax.ShapeDtypeStruct((t, N_HEADS), F32),
        jax.ShapeDtypeStruct((t, 2 * D_MODEL), BF16),
        jax.ShapeDtypeStruct((t, 2 * D_MODEL), BF16),
        jax.ShapeDtypeStruct((t, 2 * D_MODEL), BF16),
        jax.ShapeDtypeStruct((nb, seq // tm, D_MODEL, tm), BF16),
    )
    tiles_per_seq = seq // tm
    heads = pl.BlockSpec((1, N_HEADS, HEAD_DIM, tm), lambda i: (i // tiles_per_seq, 0, 0, i % tiles_per_seq))
    out_specs = (row(D_MODEL), row(D_MODEL), heads, heads, row(N_HEADS),
                 row(2 * D_MODEL), row(2 * D_MODEL), row(2 * D_MODEL),
                 pl.BlockSpec((1, 1, D_MODEL, tm), lambda i: (i // tiles_per_seq, i % tiles_per_seq, 0, 0)))
    return pl.pallas_call(
        functools.partial(_proj_prompt_kernel, tiles_per_seq=tiles_per_seq),
        grid=(nt,),
        in_specs=[row(D_MODEL)] + [_const_spec(c.shape) for c in consts],
        out_specs=out_specs,
        out_shape=out_shape,
        scratch_shapes=[pltpu.VMEM((1, LANES), F32)],
        compiler_params=pltpu.CompilerParams(dimension_semantics=("arbitrary",), vmem_limit_bytes=VMEM_LIMIT),
        name="proj_prompt",
    )(x2d, *consts)


def _project_sample(x2d, ws, bs, lng, lnb):
    t = x2d.shape[0]
    tm = min(TOKEN_TILE, t)
    nt = t // tm
    consts = list(ws) + list(bs) + [lng, lnb]
    row = lambda w: pl.BlockSpec((tm, w), lambda i: (i, 0))
    out_shape = (
        jax.ShapeDtypeStruct((t, D_MODEL), BF16),
        jax.ShapeDtypeStruct((t, D_MODEL), F32),
        jax.ShapeDtypeStruct((t, D_MODEL), F32),
        jax.ShapeDtypeStruct((t, D_MODEL), F32),
        jax.ShapeDtypeStruct((t, LANES), F32),
        jax.ShapeDtypeStruct((t, 2 * D_MODEL), BF16),
        jax.ShapeDtypeStruct((t, D_MODEL), F32),
    )
    out_specs = (row(D_MODEL), row(D_MODEL), row(D_MODEL), row(D_MODEL), row(LANES), row(2 * D_MODEL), row(D_MODEL))
    return pl.pallas_call(
        _proj_sample_kernel,
        grid=(nt,),
        in_specs=[row(D_MODEL)] + [_const_spec(c.shape) for c in consts],
        out_specs=out_specs,
        out_shape=out_shape,
        compiler_params=pltpu.CompilerParams(dimension_semantics=("arbitrary",), vmem_limit_bytes=VMEM_LIMIT),
        name="proj_sample",
    )(x2d, *consts)


def _attn_prompt_kernel(q_ref, k_ref, vt_ref, o_ref, m_ref, acc_ref, ot_ref,
                        s0, s1, p0, p1, a0, a1, c0, c1, qt_sc, *, tq, tk):
    qi = pl.program_id(2)
    nh = m_ref.shape[0]
    assert tq == 2 * tk, "the pipeline below is written for two diagonal spans per query tile"
    m_ref[...] = jnp.full(m_ref.shape, NEG_BIG, F32)
    acc_ref[...] = jnp.zeros(acc_ref.shape, F32)
    p1[...] = jnp.zeros(p1.shape, BF16)
    a1[...] = jnp.ones(a1.shape, F32)
    tv = vt_ref.shape[3]
    ones = jnp.ones((DEN_ROWS, tv), BF16)
    lanes = lambda hh: slice(hh * LANES, (hh + 1) * LANES)

    def score(t, s_slot, c_slot, transposed_q=True):
        start = pl.multiple_of(t * tk, tk)
        for hh in range(nh):
            k = k_ref[pl.ds(start, tk), lanes(hh)]
            st = _dot(k, qt_sc[hh]) if transposed_q else _dot_nt(k, q_ref[:, lanes(hh)])
            s_slot[hh] = st
            c_slot[hh] = jnp.max(st, axis=0, keepdims=True)

    def softmax(s_slot, c_slot, p_slot, a_slot, mask_offset):
        for hh in range(nh):
            if mask_offset is None:
                col_max = c_slot[hh]
            else:
                key_row = lax.broadcasted_iota(I32, (tk, tq), 0)
                qry_col = lax.broadcasted_iota(I32, (tk, tq), 1)
                s_slot[hh] = jnp.where(key_row + mask_offset <= qry_col, s_slot[hh], NEG_BIG)
                col_max = jnp.max(s_slot[hh], axis=0, keepdims=True)
            m_old = m_ref[hh]
            m_new = jnp.maximum(m_old, col_max)
            a_slot[hh] = jnp.exp2(m_old - m_new)
            for r in range(0, tk, EXP_ROWS):
                rows = slice(r, r + EXP_ROWS)
                p_slot[hh, rows, :] = jnp.exp2((s_slot[hh, rows, :] - m_new).astype(BF16))
            m_ref[hh] = m_new

    def value(t, p_slot, a_slot):
        span = jnp.maximum(t, 0)
        for hh in range(nh):
            pv = None
            for d in range(tk // tv):
                vt = vt_ref[0, span * (tk // tv) + d, hh * HEAD_DIM:(hh + 1) * HEAD_DIM, :]
                part = _dot(jnp.concatenate([vt, ones], axis=0), p_slot[hh, d * tv:(d + 1) * tv, :])
                pv = part if pv is None else pv + part
            acc_ref[hh] = a_slot[hh] * acc_ref[hh] + pv

    score(0, s0, c0, transposed_q=False)
    for hh in range(nh):
        qt_sc[hh] = q_ref[:, lanes(hh)].astype(F32).T.astype(BF16)

    def two_steps(ii):
        t = 2 * ii + 1
        value(t - 2, p1, a1)
        score(t, s1, c1)
        softmax(s0, c0, p0, a0, None)
        value(t - 1, p0, a0)
        score(t + 1, s0, c0)
        softmax(s1, c1, p1, a1, None)

    def body(jj, c):
        two_steps(2 * jj)
        two_steps(2 * jj + 1)
        return c

    lax.fori_loop(0, qi // 2, body, 0)

    @pl.when(qi % 2 == 1)
    def _():
        two_steps(qi - 1)

    t = 2 * qi + 1
    value(t - 2, p1, a1)
    score(t, s1, c1)
    softmax(s0, c0, p0, a0, 0)
    value(t - 1, p0, a0)
    softmax(s1, c1, p1, a1, tk)
    value(t, p1, a1)
    for hh in range(nh):
        ot_ref[hh * HEAD_DIM:(hh + 1) * HEAD_DIM, :] = acc_ref[hh, :HEAD_DIM] / acc_ref[hh, HEAD_DIM:HEAD_DIM + 1]
    o_ref[...] = ot_ref[...].T.astype(BF16)


def _attend_prompt(q_aug, k_aug, v_t, nb, seq):
    t = q_aug.shape[0]
    tv = v_t.shape[3]
    tq = min(Q_TILE, seq)
    tk = tq // 2
    nq = seq // tq
    nh = ATTN_HEADS
    return pl.pallas_call(
        functools.partial(_attn_prompt_kernel, tq=tq, tk=tk),
        grid=(nb, N_HEADS // nh, nq),
        in_specs=[
            pl.BlockSpec((tq, nh * LANES), lambda b, g, i: (b * nq + i, g)),
            pl.BlockSpec((seq, nh * LANES), lambda b, g, i: (b, g)),
            pl.BlockSpec((1, seq // tv, nh * HEAD_DIM, tv), lambda b, g, i: (b, 0, g, 0)),
        ],
        out_specs=pl.BlockSpec((tq, nh * HEAD_DIM), lambda b, g, i: (b * nq + i, g)),
        out_shape=jax.ShapeDtypeStruct((t, D_MODEL), BF16),
        scratch_shapes=[pltpu.VMEM((nh, 1, tq), F32), pltpu.VMEM((nh, HEAD_DIM + DEN_ROWS, tq), F32),
                        pltpu.VMEM((nh * HEAD_DIM, tq), F32),
                        pltpu.VMEM((nh, tk, tq), F32), pltpu.VMEM((nh, tk, tq), F32),
                        pltpu.VMEM((nh, tk, tq), BF16), pltpu.VMEM((nh, tk, tq), BF16),
                        pltpu.VMEM((nh, 1, tq), F32), pltpu.VMEM((nh, 1, tq), F32),
                        pltpu.VMEM((nh, 1, tq), F32), pltpu.VMEM((nh, 1, tq), F32),
                        pltpu.VMEM((nh, LANES, tq), BF16)],
        compiler_params=pltpu.CompilerParams(
            dimension_semantics=("arbitrary", "arbitrary", "arbitrary"), vmem_limit_bytes=VMEM_LIMIT),
        name="attn_prompt",
    )(q_aug, k_aug, v_t)


def _attn_sample_kernel(q_ref, k_ref, v_ref, lf_ref, ck_ref, cv_ref, clf_ref, o_ref,
                        fct_sc, fn_sc, m_sc, l_sc, acc_sc, *, n_chunks):
    c = pl.program_id(1)
    chunk = ck_ref.shape[3]
    new = q_ref.shape[1]

    @pl.when(c == 0)
    def _():
        carry = jnp.zeros((1, LANES), F32)
        for cc in range(n_chunks):
            fc = carry + _cumsum_rows(clf_ref[0, cc * chunk:(cc + 1) * chunk, :])
            carry = fc[chunk - 1:chunk, :]
            fct_sc[cc] = fc.T
        fn_sc[...] = carry + _cumsum_rows(lf_ref[0])
        m_sc[...] = jnp.full(m_sc.shape, NEG_BIG, F32)
        l_sc[...] = jnp.zeros(l_sc.shape, F32)
        acc_sc[...] = jnp.zeros(acc_sc.shape, F32)

    lane = lax.broadcasted_iota(I32, (new, LANES), 1)
    f_new = fn_sc[...]

    def attend(score, weigh, f_keys_t, mask):
        for g in range(N_GROUPS):
            qg = q_ref[0, :, g * LANES:(g + 1) * LANES]
            for hh in range(2):
                h = 2 * g + hh
                mine = (lane < HEAD_DIM) if hh == 0 else (lane >= HEAD_DIM)
                qm = jnp.where(mine, qg, 0.0).astype(BF16)
                s = score(qm, g) + f_new[:, h:h + 1] - f_keys_t[h:h + 1, :]
                if mask is not None:
                    s = jnp.where(mask, s, NEG_BIG)
                m_old = m_sc[h]
                m_new = jnp.maximum(m_old, jnp.max(s, axis=1, keepdims=True))
                alpha = jnp.exp(m_old - m_new)
                p = jnp.exp(s - m_new)
                l_sc[h] = alpha * l_sc[h] + jnp.sum(p, axis=1, keepdims=True)
                acc_sc[h] = alpha * acc_sc[h] + weigh(p.astype(BF16), g)
                m_sc[h] = m_new

    pair_t = lambda ref, g: ref[0, 2 * g:2 * g + 2].reshape(LANES, chunk).astype(BF16)
    attend(lambda qm, g: _dot(qm, pair_t(ck_ref, g)), lambda p, g: _dot_nt(p, pair_t(cv_ref, g)), fct_sc[c], None)

    @pl.when(c == n_chunks - 1)
    def _():
        causal = lax.broadcasted_iota(I32, (new, new), 1) <= lax.broadcasted_iota(I32, (new, new), 0)
        cols = lambda ref, g: ref[0, :, g * LANES:(g + 1) * LANES].astype(BF16)
        attend(lambda qm, g: _dot_nt(qm, cols(k_ref, g)), lambda p, g: _dot(p, cols(v_ref, g)), f_new.T, causal)
        for g in range(N_GROUPS):
            even = acc_sc[2 * g] / l_sc[2 * g]
            odd = acc_sc[2 * g + 1] / l_sc[2 * g + 1]
            o_ref[:, g * LANES:(g + 1) * LANES] = jnp.where(lane < HEAD_DIM, even, odd).astype(BF16)


def _attend_sample(q, k, v, lf, cache_k, cache_v, cache_lf):
    nb, new, _ = q.shape
    past = cache_k.shape[3]
    chunk = min(SAMPLE_CHUNK, past)
    n_chunks = past // chunk
    blk = lambda n, w: pl.BlockSpec((1, n, w), lambda b, c: (b, 0, 0))
    cache = pl.BlockSpec((1, N_HEADS, HEAD_DIM, chunk), lambda b, c: (b, 0, 0, c))
    return pl.pallas_call(
        functools.partial(_attn_sample_kernel, n_chunks=n_chunks),
        grid=(nb, n_chunks),
        in_specs=[blk(new, D_MODEL), blk(new, D_MODEL), blk(new, D_MODEL), blk(new, LANES),
                  cache, cache, blk(past, LANES)],
        out_specs=pl.BlockSpec((new, D_MODEL), lambda b, c: (b, 0)),
        out_shape=jax.ShapeDtypeStruct((nb * new, D_MODEL), BF16),
        scratch_shapes=[pltpu.VMEM((n_chunks, LANES, chunk), F32), pltpu.VMEM((new, LANES), F32),
                        pltpu.VMEM((N_HEADS, new, 1), F32), pltpu.VMEM((N_HEADS, new, 1), F32),
                        pltpu.VMEM((N_HEADS, new, LANES), F32)],
        compiler_params=pltpu.CompilerParams(dimension_semantics=("arbitrary", "arbitrary"),
                                             vmem_limit_bytes=VMEM_LIMIT),
        name="attn_sample",
    )(q, k, v, lf, cache_k, cache_v, cache_lf)


def _post_kernel(x_ref, u_ref, va_ref, g_ref, at_ref, p_ref, base_ref, ws_ref, bst_ref, wpa, wpb, wo,
                 l1g, l1b, wr_both, br, wpg, bpg, wpp,
                 x1_o, resid_o, topi_o, gate_o, rank_o, cnt_o, a_sc, base_sc, *, chunk, dn_alpha):
    i = pl.program_id(0)
    tm = x_ref.shape[0]

    @pl.when(i == 0)
    def _():
        base_sc[...] = base_ref[...]

    tri = _tri(chunk, strict=False)
    zeros = jnp.zeros((chunk, LANES), BF16)
    lane2 = lax.broadcasted_iota(I32, (chunk, 2 * LANES), 1)
    for g in range(0, N_GROUPS, 2):
        sl = slice(g * LANES, (g + 2) * LANES)
        wm = jnp.concatenate([ws_ref[g, :chunk, :chunk] * tri, ws_ref[g + 1, :chunk, :chunk] * tri], axis=1).astype(BF16)
        bias = jnp.where(lane2 < LANES, bst_ref[:chunk, g:g + 1], bst_ref[:chunk, g + 1:g + 2])
        for c in range(tm // chunk):
            rows = slice(c * chunk, (c + 1) * chunk)
            va = va_ref[rows, sl]
            diag = jnp.concatenate([jnp.concatenate([va[:, :LANES], zeros], axis=1),
                                    jnp.concatenate([zeros, va[:, LANES:]], axis=1)], axis=0)
            sg = _dot(wm, diag) + bias
            a_sc[rows, sl] = (u_ref[rows, sl].astype(F32) * sg).astype(BF16)

    pa = _dot(a_sc[...], wpa[...])
    pb = _dot(at_ref[...], wpb[...])
    merged = g_ref[:, :D_MODEL].astype(F32) * pa + g_ref[:, D_MODEL:].astype(F32) * pb
    x1 = _layer_norm(dn_alpha * x_ref[...] + _dot(merged.astype(BF16), wo[...]), l1g[...], l1b[...])
    x1b = x1.astype(BF16)
    x1_o[...] = x1

    ple = jax.nn.sigmoid(_dot(x1b, wpg[...]) + bpg[...]) * _dot(p_ref[...].astype(BF16), wpp[...])
    resid_o[...] = dn_alpha * x1 + ple

    x1_lo = (x1 - x1b.astype(F32)).astype(BF16)
    both = _dot(x1b, wr_both[...])
    logits = (both[:, :LANES] + (both[:, LANES:] + _dot(x1_lo, wr_both[:, :LANES]))) + br[...]
    lane = lax.broadcasted_iota(I32, (tm, LANES), 1)
    work = logits
    vals, idxs = [], []
    for _ in range(TOP_K):
        mk = jnp.max(work, axis=1, keepdims=True)
        ik = jnp.min(jnp.where(work == mk, lane, LANES), axis=1, keepdims=True)
        vals.append(mk)
        idxs.append(ik)
        work = jnp.where(lane == ik, -jnp.inf, work)
    exps = [jnp.exp(v - vals[0]) for v in vals]
    den = exps[0] + exps[1] + exps[2] + exps[3]

    lower = _tri(tm, strict=True).astype(BF16)
    run = base_sc[...]
    topi = jnp.zeros((tm, LANES), I32)
    gate = jnp.zeros((tm, LANES), F32)
    rank = jnp.zeros((tm, LANES), F32)
    for k in range(TOP_K):
        onehot = jnp.where(lane == idxs[k], 1.0, 0.0)
        before = _dot(lower, onehot.astype(BF16))
        rk = jnp.sum(onehot * (before + run), axis=1, keepdims=True)
        topi = jnp.where(lane == k, idxs[k], topi)
        gate = jnp.where(lane == k, exps[k] / den, gate)
        rank = jnp.where(lane == k, rk, rank)
        run = run + jnp.sum(onehot, axis=0, keepdims=True)
    base_sc[...] = run
    topi_o[...] = topi
    gate_o[...] = gate
    rank_o[...] = rank.astype(I32)
    cnt_o[...] = run


def _post_attention(x2d, u, va, gates, attn, p2d, base, consts, chunk, dn_alpha):
    t = x2d.shape[0]
    tm = min(TOKEN_TILE, t)
    nt = t // tm
    row = lambda w: pl.BlockSpec((tm, w), lambda i: (i, 0))
    ple_dim = p2d.shape[1]
    out_shape = (
        jax.ShapeDtypeStruct((t, D_MODEL), F32),
        jax.ShapeDtypeStruct((t, D_MODEL), F32),
        jax.ShapeDtypeStruct((t, LANES), I32),
        jax.ShapeDtypeStruct((t, LANES), F32),
        jax.ShapeDtypeStruct((t, LANES), I32),
        jax.ShapeDtypeStruct((1, LANES), F32),
    )
    out_specs = (row(D_MODEL), row(D_MODEL), row(LANES), row(LANES), row(LANES),
                 pl.BlockSpec((1, LANES), lambda i: (0, 0)))
    return pl.pallas_call(
        functools.partial(_post_kernel, chunk=chunk, dn_alpha=dn_alpha),
        grid=(nt,),
        in_specs=[row(D_MODEL), row(D_MODEL), row(D_MODEL), row(2 * D_MODEL), row(D_MODEL), row(ple_dim),
                  _const_spec(base.shape)] + [_const_spec(c.shape) for c in consts],
        out_specs=out_specs,
        out_shape=out_shape,
        scratch_shapes=[pltpu.VMEM((tm, D_MODEL), BF16), pltpu.VMEM((1, LANES), F32)],
        compiler_params=pltpu.CompilerParams(dimension_semantics=("arbitrary",), vmem_limit_bytes=VMEM_LIMIT),
        name="post_attention",
    )(x2d, u, va, gates, attn, p2d, base, *consts)


def _dispatch_kernel(fill_start_ref, fill_len_ref, nu_ref, xp_ref, xn_ref, dest_ref, xs_out, zero_sc, sem, zsem,
                     *, prompt_steps, n_blocks):
    i = pl.program_id(0)
    td = xp_ref.shape[0]

    @pl.when(i == 0)
    def _():
        zero_sc[...] = jnp.zeros(zero_sc.shape, zero_sc.dtype)

        def zero_fill(wait):
            def go(copy):
                copy.wait() if wait else copy.start()

            def per_expert(e, c):
                n = fill_len_ref[e]
                base = fill_start_ref[e]
                head = jnp.bitwise_and(n, SUBLANES - 1)
                for r in range(SUBLANES - 1):
                    @pl.when(r < head)
                    def _(r=r):
                        go(pltpu.make_async_copy(zero_sc.at[pl.ds(0, 1)], xs_out.at[pl.ds(base + r, 1)], zsem))

                piece = EXPERT_BLOCK // 2
                while piece >= SUBLANES:
                    offset = pl.multiple_of(base + head + jnp.bitwise_and(n, -(2 * piece)), SUBLANES)

                    @pl.when(jnp.bitwise_and(n, piece) != 0)
                    def _(piece=piece, offset=offset):
                        go(pltpu.make_async_copy(zero_sc.at[pl.ds(0, piece)], xs_out.at[pl.ds(offset, piece)], zsem))

                    piece //= 2
                return c

            lax.fori_loop(0, N_EXPERTS, per_expert, 0)

            def per_block(j, c):
                row0 = pl.multiple_of(j * EXPERT_BLOCK, EXPERT_BLOCK)
                go(pltpu.make_async_copy(zero_sc, xs_out.at[pl.ds(row0, EXPERT_BLOCK)], zsem))
                return c

            lax.fori_loop(nu_ref[0], n_blocks, per_block, 0)

        zero_fill(wait=False)
        zero_fill(wait=True)

    def scatter(x_ref):
        def body(j, c):
            for u in range(DMA_UNROLL):
                tok = j * (DMA_UNROLL // TOP_K) + u // TOP_K
                pltpu.make_async_copy(x_ref.at[pl.ds(tok, 1)], xs_out.at[pl.ds(dest_ref[j * DMA_UNROLL + u], 1)],
                                      sem).start(priority=u % 2)
            return c

        lax.fori_loop(0, td * TOP_K // DMA_UNROLL, body, 0)
        for _ in range(TOP_K):
            pltpu.make_async_copy(x_ref, xs_out.at[pl.ds(0, td)], sem).wait()

    @pl.when(i < prompt_steps)
    def _():
        scatter(xp_ref)

    @pl.when(i >= prompt_steps)
    def _():
        scatter(xn_ref)


def _dispatch(fill_start, fill_len, n_used, x1_prompt, x1_sample, dest, n_rows):
    tp, w = x1_prompt.shape
    ts = x1_sample.shape[0]
    td = min(DISPATCH_TILE, ts)
    prompt_steps, sample_steps = tp // td, ts // td
    grid_spec = pltpu.PrefetchScalarGridSpec(
        num_scalar_prefetch=3,
        grid=(prompt_steps + sample_steps,),
        in_specs=[pl.BlockSpec((td, w), lambda i, *_: (jnp.minimum(i, prompt_steps - 1), 0)),
                  pl.BlockSpec((td, w), lambda i, *_: (jnp.maximum(i - prompt_steps, 0), 0)),
                  pl.BlockSpec((td * TOP_K,), lambda i, *_: (i,), memory_space=pltpu.SMEM)],
        out_specs=pl.BlockSpec(memory_space=pl.ANY),
        scratch_shapes=[pltpu.VMEM((EXPERT_BLOCK, w), x1_prompt.dtype),
                        pltpu.SemaphoreType.DMA(()), pltpu.SemaphoreType.DMA(())],
    )
    return pl.pallas_call(
        functools.partial(_dispatch_kernel, prompt_steps=prompt_steps, n_blocks=n_rows // EXPERT_BLOCK),
        grid_spec=grid_spec,
        out_shape=jax.ShapeDtypeStruct((n_rows, w), x1_prompt.dtype),
        compiler_params=pltpu.CompilerParams(dimension_semantics=("arbitrary",), vmem_limit_bytes=VMEM_LIMIT),
        name="dispatch",
    )(fill_start, fill_len, n_used, x1_prompt, x1_sample, dest)


def _expert_kernel(be_ref, nu_ref, next_ref, slot_ref, xs_ref, wgu, bgu, wdn, bdn, ys_o,
                   wgu_in, wdn_in, wgu_bf, wdn_bf, sem):
    j = pl.program_id(0)
    used = j < nu_ref[0]
    new_expert = jnp.logical_or(j == 0, be_ref[j] != be_ref[jnp.maximum(j - 1, 0)])
    slot = slot_ref[j]

    def weight_copies(e, s):
        return (pltpu.make_async_copy(wgu.at[e], wgu_in.at[s], sem.at[0, s]),
                pltpu.make_async_copy(wdn.at[e], wdn_in.at[s], sem.at[1, s]))

    @pl.when(j == 0)
    def _():
        for cp in weight_copies(be_ref[0], slot):
            cp.start()

    @pl.when(jnp.logical_not(used))
    def _():
        ys_o[...] = jnp.zeros(ys_o.shape, ys_o.dtype)

    @pl.when(jnp.logical_and(used, new_expert))
    def _():
        for cp in weight_copies(be_ref[j], slot):
            cp.wait()
        wgu_bf[...] = wgu_in[slot].astype(BF16)
        wdn_bf[...] = wdn_in[slot].astype(BF16)
        nxt = next_ref[j]

        @pl.when(nxt >= 0)
        def _():
            for cp in weight_copies(nxt, 1 - slot):
                cp.start()

    @pl.when(used)
    def _():
        h = _dot(xs_ref[...].astype(BF16), wgu_bf[...]) + bgu[0]
        d_e = h.shape[1] // 2
        gate = jnp.minimum(h[:, :d_e], SWIGLU_LIMIT)
        up = jnp.clip(h[:, d_e:], -SWIGLU_LIMIT, SWIGLU_LIMIT)
        glu = gate * jax.nn.sigmoid(gate * SWIGLU_ALPHA)
        ys_o[...] = _dot(((up + 1.0) * glu).astype(BF16), wdn_bf[...]) + bdn[0]


def _experts(block_expert, n_used, next_expert, block_slot, xs, wgu, bgu, wdn, bdn):
    rows, w = xs.shape
    nblk = rows // EXPERT_BLOCK
    d_in, d_gu = wgu.shape[1], wgu.shape[2]
    d_e, d_out = wdn.shape[1], wdn.shape[2]
    grid_spec = pltpu.PrefetchScalarGridSpec(
        num_scalar_prefetch=4,
        grid=(nblk,),
        in_specs=[
            pl.BlockSpec((EXPERT_BLOCK, w), lambda j, *_: (j, 0)),
            pl.BlockSpec(memory_space=pl.ANY),
            pl.BlockSpec((1, 1, d_gu), lambda j, be, *_: (be[j], 0, 0)),
            pl.BlockSpec(memory_space=pl.ANY),
            pl.BlockSpec((1, 1, d_out), lambda j, be, *_: (be[j], 0, 0)),
        ],
        out_specs=pl.BlockSpec((EXPERT_BLOCK, d_out), lambda j, *_: (j, 0)),
        scratch_shapes=[pltpu.VMEM((2, d_in, d_gu), F32), pltpu.VMEM((2, d_e, d_out), F32),
                        pltpu.VMEM((d_in, d_gu), BF16), pltpu.VMEM((d_e, d_out), BF16),
                        pltpu.SemaphoreType.DMA((2, 2))],
    )
    return pl.pallas_call(
        _expert_kernel,
        grid_spec=grid_spec,
        out_shape=jax.ShapeDtypeStruct((rows, d_out), F32),
        compiler_params=pltpu.CompilerParams(dimension_semantics=("arbitrary",), vmem_limit_bytes=VMEM_LIMIT),
        name="experts",
    )(block_expert, n_used, next_expert, block_slot, xs, wgu, bgu, wdn, bdn)


def _final_kernel(resid_ref, gate_ref, dest_ref, dest_next_ref, ys_ref, lng, lnb, y_o, buf, sem, *, n):
    i = pl.program_id(0)
    tg = resid_ref.shape[0]

    def gather(idx_ref, slot):
        def body(j, c):
            for u in range(DMA_UNROLL):
                tok = j * (DMA_UNROLL // TOP_K) + u // TOP_K
                pltpu.make_async_copy(ys_ref.at[pl.ds(idx_ref[j * DMA_UNROLL + u], 1)],
                                      buf.at[slot, u % TOP_K, pl.ds(tok, 1)], sem.at[slot]).start(priority=u % 2)
            return c

        lax.fori_loop(0, tg * TOP_K // DMA_UNROLL, body, 0)

    @pl.when(i == 0)
    def _():
        gather(dest_ref, 0)

    @pl.when(i + 1 < n)
    def _():
        gather(dest_next_ref, (i + 1) % 2)

    slot = i % 2
    for k in range(TOP_K):
        pltpu.make_async_copy(ys_ref.at[pl.ds(0, tg)], buf.at[slot, k], sem.at[slot]).wait()
    acc = resid_ref[...]
    for k in range(TOP_K):
        acc = acc + gate_ref[:, k:k + 1] * buf[slot, k]
    y_o[...] = _layer_norm(acc, lng[...], lnb[...])


def _combine(resid, gate, dest, ys, lng, lnb):
    t = resid.shape[0]
    tg = min(COMBINE_TILE, t)
    nt = t // tg
    row = lambda w: pl.BlockSpec((tg, w), lambda i: (i, 0))
    return pl.pallas_call(
        functools.partial(_final_kernel, n=nt),
        grid=(nt,),
        in_specs=[row(D_MODEL), row(LANES),
                  pl.BlockSpec((tg * TOP_K,), lambda i: (i,), memory_space=pltpu.SMEM),
                  pl.BlockSpec((tg * TOP_K,), lambda i: (jnp.minimum(i + 1, nt - 1),), memory_space=pltpu.SMEM),
                  pl.BlockSpec(memory_space=pl.ANY),
                  _const_spec(lng.shape), _const_spec(lnb.shape)],
        out_specs=row(D_MODEL),
        out_shape=jax.ShapeDtypeStruct((t, D_MODEL), F32),
        scratch_shapes=[pltpu.VMEM((2, TOP_K, tg, ys.shape[1]), F32), pltpu.SemaphoreType.DMA((2,))],
        compiler_params=pltpu.CompilerParams(dimension_semantics=("arbitrary",), vmem_limit_bytes=VMEM_LIMIT),
        name="combine",
    )(resid, gate, dest, dest, ys, lng, lnb)


def _layer(x_prompt, x_sample, cache_k, cache_v, cache_logf, p_prompt, p_sample,
           w_in, b_in, lnv_g, lnv_b, w_s, b_s, w_pa, w_pb, w_o, ln1_g, ln1_b,
           w_router, b_router, w_gu, b_gu, w_dn, b_dn, w_ple_gate, b_ple_gate, w_ple_proj, ln2_g, ln2_b,
           dn_alpha):
    nb, seq, d = x_prompt.shape
    sb, new, _ = x_sample.shape
    past = cache_k.shape[1]
    tp, ts = nb * seq, sb * new
    vec = lambda a: a.reshape(1, -1).astype(F32)

    ws, bs = _proj_weights(w_in, b_in)
    lng, lnb = vec(lnv_g), vec(lnv_b)
    xp = x_prompt.reshape(tp, d)
    xsm = x_sample.reshape(ts, d)
    u_p, va_p, k_p, v_p, lf_p, g_p, qa_p, ka_p, vt_p = _project_prompt(xp, seq, ws, bs, lng, lnb)
    u_s, va_s, k_s, v_s, lf_s, g_s, q_s = _project_sample(xsm, ws, bs, lng, lnb)

    attn_p = _attend_prompt(qa_p, ka_p, vt_p, nb, seq)
    cache_lf = jnp.pad(cache_logf.astype(F32), ((0, 0), (0, 0), (0, LANES - N_HEADS)))
    rows_last = lambda a: jnp.transpose(a, (0, 2, 3, 1))
    attn_s = _attend_sample(q_s.reshape(sb, new, d), k_s.reshape(sb, new, d), v_s.reshape(sb, new, d),
                            lf_s.reshape(sb, new, LANES), rows_last(cache_k), rows_last(cache_v), cache_lf)

    wr = jnp.pad(w_router.astype(F32), ((0, 0), (0, LANES - N_EXPERTS)))
    wr_hi = wr.astype(BF16)
    wr_lo = (wr - wr_hi.astype(F32)).astype(BF16)
    br = jnp.concatenate([b_router.astype(F32), jnp.full((LANES - N_EXPERTS,), NEG_BIG, F32)]).reshape(1, LANES)
    spatial = b_s.shape[1]
    bst = jnp.pad(b_s.astype(F32).T, ((0, 0), (0, LANES - b_s.shape[0])))
    post_consts = [w_s.astype(F32), bst, w_pa.astype(BF16), w_pb.astype(BF16), w_o.astype(BF16),
                   vec(ln1_g), vec(ln1_b), jnp.concatenate([wr_hi, wr_lo], axis=1), br,
                   w_ple_gate.astype(BF16), vec(b_ple_gate), w_ple_proj.astype(BF16)]
    base0 = jnp.zeros((1, LANES), F32)
    x1r_p, resid_p, topi_p, gate_p, rank_p, cnt_p = _post_attention(
        xp, u_p, va_p, g_p, attn_p, p_prompt.reshape(tp, -1), base0, post_consts, min(spatial, seq), dn_alpha)
    x1r_s, resid_s, topi_s, gate_s, rank_s, cnt = _post_attention(
        xsm, u_s, va_s.astype(BF16), g_s, attn_s, p_sample.reshape(ts, -1), cnt_p, post_consts,
        min(spatial, new), dn_alpha)

    counts = cnt[0, :N_EXPERTS].astype(I32)
    padded = (counts + EXPERT_BLOCK - 1) // EXPERT_BLOCK * EXPERT_BLOCK
    pad_end = jnp.cumsum(padded)
    pad_start = (pad_end - padded).astype(I32)
    n_assign = (tp + ts) * TOP_K
    nblk = -(-n_assign // EXPERT_BLOCK) + N_EXPERTS
    n_rows = nblk * EXPERT_BLOCK
    block_row0 = jnp.arange(nblk, dtype=I32) * EXPERT_BLOCK
    block_expert = jnp.minimum(jnp.sum(pad_end[None, :] <= block_row0[:, None], axis=1), N_EXPERTS - 1).astype(I32)
    n_used = (pad_end[-1:] // EXPERT_BLOCK).astype(I32)

    dest_of = lambda topi, rank: (jnp.take(pad_start, topi[:, :TOP_K]) + rank[:, :TOP_K]).reshape(-1)
    dest_p, dest_s = dest_of(topi_p, rank_p), dest_of(topi_s, rank_s)
    xs = _dispatch(pad_start + counts, padded - counts, n_used, x1r_p, x1r_s,
                   jnp.concatenate([dest_p, dest_s]), n_rows)

    expert_ids = jnp.arange(N_EXPERTS, dtype=I32)
    present = padded > 0
    later = jnp.where(jnp.logical_and(present[None, :], expert_ids[None, :] > expert_ids[:, None]),
                      expert_ids[None, :], N_EXPERTS)
    next_present = jnp.min(later, axis=1)
    next_present = jnp.where(next_present == N_EXPERTS, -1, next_present).astype(I32)
    order = (jnp.cumsum(present.astype(I32)) - 1).astype(I32)
    ys = _experts(block_expert, n_used, jnp.take(next_present, block_expert), jnp.take(order, block_expert) % 2,
                  xs, w_gu.astype(F32), b_gu.astype(F32)[:, None, :], w_dn.astype(F32), b_dn.astype(F32)[:, None, :])

    l2g, l2b = vec(ln2_g), vec(ln2_b)
    y_p = _combine(resid_p, gate_p, dest_p, ys, l2g, l2b)
    y_s = _combine(resid_s, gate_s, dest_s, ys, l2g, l2b)

    shp = lambda a, b_, n: a.reshape(b_, n, N_HEADS, HEAD_DIM)
    rows_first = lambda a: jnp.transpose(a, (0, 3, 1, 2))
    return (y_p.reshape(nb, seq, d), y_s.reshape(sb, new, d),
            rows_first(k_p), rows_first(v_p), lf_p.reshape(nb, seq, N_HEADS),
            shp(k_s, sb, new), shp(v_s, sb, new), lf_s[:, :N_HEADS].reshape(sb, new, N_HEADS),
            va_s.reshape(sb, new, N_GROUPS, d // N_GROUPS))


def kernel(x_prompt, x_sample, cache_fox_k, cache_fox_v, cache_fox_logf, p_prompt, p_sample, w_in, b_in, lnv_g, lnv_b, w_s, b_s, w_pa, w_pb, w_o, ln1_g, ln1_b, w_router, b_router, w_gu, b_gu, w_dn, b_dn, w_ple_gate, b_ple_gate, w_ple_proj, ln2_g, ln2_b):
    depth = w_in.shape[0]
    assert depth == 1, "the layer loop below carries one layer"
    dn_alpha = float((2 * depth) ** 0.25)
    outs = _layer(x_prompt, x_sample, cache_fox_k[0], cache_fox_v[0], cache_fox_logf[0], p_prompt[0], p_sample[0],
                  w_in[0], b_in[0], lnv_g[0], lnv_b[0], w_s[0], b_s[0], w_pa[0], w_pb[0], w_o[0], ln1_g[0], ln1_b[0],
                  w_router[0], b_router[0], w_gu[0], b_gu[0], w_dn[0], b_dn[0], w_ple_gate[0], b_ple_gate[0],
                  w_ple_proj[0], ln2_g[0], ln2_b[0], dn_alpha)
    y_p, y_s, k_p, v_p, lf_p, k_s, v_s, lf_s, va_s = outs
    lead = lambda a: a[None]
    return (y_p, y_s, lead(k_p), lead(v_p), lead(lf_p), lead(k_s), lead(v_s), lead(lf_s), lead(va_s))
```
